```python
import jax, jax.numpy as jnp
from jax import lax
import numpy as np

D_MODEL = 2048
BATCH = 8
SEQ = 2048
DEPTH = 4

N_HEADS = 16
HEAD_DIM = D_MODEL // N_HEADS
CONV_WIDTH = 31
D_FF = ((8 * D_MODEL // 3 + 255) // 256) * 256
N_EXPERTS = 8
TOP_K = 2
D_FF_EXPERT = D_FF
Q_BLOCK = 128
N_A_LAYERS = DEPTH // 2
N_B_LAYERS = DEPTH - N_A_LAYERS
N_DENSE_LAYERS = (DEPTH + 1) // 2
N_MOE_LAYERS = DEPTH // 2
EPS = 1e-6

kernel_name = "yoco_conformer_fox_moe_adaln"


def _rms_norm(x, g):
    xf = x.astype(jnp.float32)
    y = xf * lax.rsqrt(jnp.mean(xf * xf, axis=-1, keepdims=True) + EPS)
    return (y * g.astype(jnp.float32)).astype(x.dtype)


def _layer_norm(x, g, b):
    xf = x.astype(jnp.float32)
    mu = jnp.mean(xf, axis=-1, keepdims=True)
    var = jnp.mean(jnp.square(xf - mu), axis=-1, keepdims=True)
    y = (xf - mu) * lax.rsqrt(var + EPS) * g.astype(jnp.float32) + b.astype(jnp.float32)
    return y.astype(x.dtype)


def _modulate(h, shift, scale):
    return h * (1 + scale[:, None, :]) + shift[:, None, :]


def _conformer_conv(h, pw1_w, pw1_b, dw_w, dw_b, ln_g, ln_b, pw2_w, pw2_b):
    d = h.shape[-1]
    a = h @ pw1_w + pw1_b
    u = a[..., :d] * jax.nn.sigmoid(a[..., d:])
    u = lax.conv_general_dilated(
        u, dw_w[:, None, :], window_strides=(1,), padding=[(CONV_WIDTH - 1, 0)],
        dimension_numbers=("NWC", "WIO", "NWC"), feature_group_count=d) + dw_b
    u = jax.nn.silu(_layer_norm(u, ln_g, ln_b))
    return u @ pw2_w + pw2_b


def _shared_kv(x, c_act, kv_ada_w, kv_ada_b, kv_norm_g, w_kvf, b_f):
    b, s, d = x.shape
    shift, scale = jnp.split(c_act @ kv_ada_w + kv_ada_b, 2, axis=-1)
    h = _modulate(_rms_norm(x, kv_norm_g), shift, scale)
    kvf = h @ w_kvf
    k = kvf[..., :d].reshape(b, s, N_HEADS, HEAD_DIM).transpose(0, 2, 1, 3)
    v = kvf[..., d:2 * d].reshape(b, s, N_HEADS, HEAD_DIM).transpose(0, 2, 1, 3)
    log_f = jax.nn.log_sigmoid((kvf[..., 2 * d:] + b_f).astype(jnp.float32))
    cum = jnp.cumsum(log_f, axis=1).transpose(0, 2, 1)
    return k, v, cum


def _forgetting_attention(h, wq, wo, k, v, cum):
    b, s, d = h.shape
    q = (h @ wq).reshape(b, s, N_HEADS, HEAD_DIM).transpose(0, 2, 1, 3)
    scale = HEAD_DIM ** -0.5
    outs = []
    for i in range(s // Q_BLOCK):
        lo, hi = i * Q_BLOCK, (i + 1) * Q_BLOCK
        logits = jnp.einsum("bhqd,bhkd->bhqk", q[:, :, lo:hi], k[:, :, :hi]).astype(jnp.float32) * scale
        logits = logits + cum[:, :, lo:hi, None] - cum[:, :, None, :hi]
        causal = (lo + jnp.arange(Q_BLOCK))[:, None] >= jnp.arange(hi)[None, :]
        logits = jnp.where(causal, logits, -jnp.inf)
        p = jax.nn.softmax(logits, axis=-1).astype(v.dtype)
        outs.append(jnp.einsum("bhqk,bhkd->bhqd", p, v[:, :, :hi]))
    o = jnp.concatenate(outs, axis=2).transpose(0, 2, 1, 3).reshape(b, s, d)
    return o @ wo


def _swiglu(h, wg, wu, wd):
    return (jax.nn.silu(h @ wg) * (h @ wu)) @ wd


def _moe_swiglu(h, router_w, router_b, wg, wu, wd):
    b, s, d = h.shape
    t = h.reshape(b * s, d)
    logits = (t @ router_w + router_b).astype(jnp.float32)
    top_v, top_i = lax.top_k(logits, TOP_K)
    top_w = jax.nn.softmax(top_v, axis=-1)
    gates = jnp.einsum("tk,tke->te", top_w, jax.nn.one_hot(top_i, N_EXPERTS, dtype=jnp.float32)).astype(h.dtype)
    y = jnp.zeros_like(t)
    for e in range(N_EXPERTS):
        y = y + gates[:, e:e + 1] * _swiglu(t, wg[e], wu[e], wd[e])
    return y.reshape(b, s, d)


def setup_inputs(seed: int = 0) -> dict:
    key = jax.random.key(seed)
    ks = iter(jax.random.split(key, 40))
    f32 = jnp.float32
    D, H, F, E, Fe = D_MODEL, N_HEADS, D_FF, N_EXPERTS, D_FF_EXPERT

    def nrm(shape, std):
        return std * jax.random.normal(next(ks), shape, f32)

    def gain(shape):
        return 1.0 + nrm(shape, 0.02)

    return {
        "x": nrm((BATCH, SEQ, D), 1.0),
        "c": nrm((BATCH, D), 1.0),
        "ada_w": nrm((DEPTH, D, 6 * D), 0.5 * D ** -0.5),
        "ada_b": nrm((DEPTH, 6 * D), 0.02),
        "norm_mix_g": gain((DEPTH, D)),
        "norm_ffn_g": gain((DEPTH, D)),
        "conv_pw1_w": nrm((N_A_LAYERS, D, 2 * D), D ** -0.5),
        "conv_pw1_b": nrm((N_A_LAYERS, 2 * D), 0.02),
        "conv_dw_w": nrm((N_A_LAYERS, CONV_WIDTH, D), CONV_WIDTH ** -0.5),
        "conv_dw_b": nrm((N_A_LAYERS, D), 0.02),
        "conv_ln_g": gain((N_A_LAYERS, D)),
        "conv_ln_b": nrm((N_A_LAYERS, D), 0.02),
        "conv_pw2_w": nrm((N_A_LAYERS, D, D), D ** -0.5),
        "conv_pw2_b": nrm((N_A_LAYERS, D), 0.02),
        "kv_ada_w": nrm((D, 2 * D), 0.5 * D ** -0.5),
        "kv_ada_b": nrm((2 * D,), 0.02),
        "kv_norm_g": gain((D,)),
        "w_kvf": nrm((D, 2 * D + H), D ** -0.5),
        "b_f": 2.0 + 3.0 * jax.random.uniform(next(ks), (H,), f32),
        "attn_wq": nrm((N_B_LAYERS, D, D), D ** -0.5),
        "attn_wo": nrm((N_B_LAYERS, D, D), D ** -0.5),
        "ffn_w_gate": nrm((N_DENSE_LAYERS, D, F), D ** -0.5),
        "ffn_w_up": nrm((N_DENSE_LAYERS, D, F), D ** -0.5),
        "ffn_w_down": nrm((N_DENSE_LAYERS, F, D), F ** -0.5),
        "moe_router_w": nrm((N_MOE_LAYERS, D, E), D ** -0.5),
        "moe_router_b": nrm((N_MOE_LAYERS, E), 0.01),
        "moe_w_gate": nrm((N_MOE_LAYERS, E, D, Fe), D ** -0.5),
        "moe_w_up": nrm((N_MOE_LAYERS, E, D, Fe), D ** -0.5),
        "moe_w_down": nrm((N_MOE_LAYERS, E, Fe, D), Fe ** -0.5),
        "final_norm_g": gain((D,)),
    }


def reference(x, c, ada_w, ada_b, norm_mix_g, norm_ffn_g,
              conv_pw1_w, conv_pw1_b, conv_dw_w, conv_dw_b, conv_ln_g, conv_ln_b, conv_pw2_w, conv_pw2_b,
              kv_ada_w, kv_ada_b, kv_norm_g, w_kvf, b_f,
              attn_wq, attn_wo,
              ffn_w_gate, ffn_w_up, ffn_w_down,
              moe_router_w, moe_router_b, moe_w_gate, moe_w_up, moe_w_down,
              final_norm_g):
    c_act = jax.nn.silu(c)
    k_sh = v_sh = cum_sh = None
    for layer in range(DEPTH):
        if layer == N_A_LAYERS:
            k_sh, v_sh, cum_sh = _shared_kv(x, c_act, kv_ada_w, kv_ada_b, kv_norm_g, w_kvf, b_f)
        mod = c_act @ ada_w[layer] + ada_b[layer]
        sh1, sc1, g1, sh2, sc2, g2 = jnp.split(mod, 6, axis=-1)

        h = _modulate(_rms_norm(x, norm_mix_g[layer]), sh1, sc1)
        if layer < N_A_LAYERS:
            i = layer
            y = _conformer_conv(h, conv_pw1_w[i], conv_pw1_b[i], conv_dw_w[i], conv_dw_b[i],
                                conv_ln_g[i], conv_ln_b[i], conv_pw2_w[i], conv_pw2_b[i])
        else:
            i = layer - N_A_LAYERS
            y = _forgetting_attention(h, attn_wq[i], attn_wo[i], k_sh, v_sh, cum_sh)
        x = x + g1[:, None, :] * y

        h = _modulate(_rms_norm(x, norm_ffn_g[layer]), sh2, sc2)
        j = layer // 2
        if layer % 2 == 0:
            y = _swiglu(h, ffn_w_gate[j], ffn_w_up[j], ffn_w_down[j])
        else:
            y = _moe_swiglu(h, moe_router_w[j], moe_router_b[j], moe_w_gate[j], moe_w_up[j], moe_w_down[j])
        x = x + g2[:, None, :] * y
    return _rms_norm(x, final_norm_g)
```

```python
import functools

import jax
import jax.numpy as jnp
from jax import lax
from jax.experimental import pallas as pl
from jax.experimental.pallas import tpu as pltpu

F32 = jnp.float32
BF16 = jnp.bfloat16
U32 = jnp.uint32
I32 = jnp.int32

EPS = 1e-6
TOP_K = 2
LANES = 128
MXU_DIM = 256
HALO_ROWS = 32
HI_MASK = 0xFFFF0000

TILES = dict(
    norm_rows=512,
    mm_rows=1024, mm_cols=512,
    res_rows=256,
    attn_q=512,
    ffn_rows=512, ffn_cols=512,
    moe_rows=1024, moe_cols=256,
    route_rows=512,
    dispatch_rows=512,
    combine_rows=256,
    ada_cols=1024,
    cum_rows=256,
    epi_chunk=128,
)
VMEM_LIMIT = 56 * 2**20


def _cparams(n_axes, vmem=VMEM_LIMIT):
    return pltpu.CompilerParams(dimension_semantics=("arbitrary",) * n_axes,
                                vmem_limit_bytes=vmem)


def _sigmoid(v):
    return 1.0 / (1.0 + jnp.exp(-v))


def _silu(v):
    return v * _sigmoid(v)


def _pack_halves(v):
    n = v.shape[1] // 2
    hi = lax.bitcast_convert_type(v[:, :n].astype(BF16).astype(F32), U32)
    lo = lax.bitcast_convert_type(v[:, n:].astype(BF16).astype(F32), U32)
    return hi | lax.shift_right_logical(lo, jnp.uint32(16))


def _unpack_halves(p):
    hi = lax.bitcast_convert_type(p & jnp.uint32(HI_MASK), F32)
    lo = lax.bitcast_convert_type(lax.shift_left(p, jnp.uint32(16)), F32)
    return hi, lo


def _norm_outputs(xn, norm_refs, out_refs):
    ms = jnp.mean(xn * xn, axis=-1, keepdims=True)
    xr = xn * lax.rsqrt(ms + EPS)
    for k, o_ref in enumerate(out_refs):
        g = norm_refs[3 * k][...]
        sc = norm_refs[3 * k + 1][0]
        sh = norm_refs[3 * k + 2][0]
        o_ref[...] = (xr * g * (1.0 + sc) + sh).astype(o_ref.dtype)


def _ada_kernel(c_ref, w_ref, b_ref, o_ref):
    ca = _silu(c_ref[...]).astype(BF16)
    o_ref[0] = jnp.dot(ca, w_ref[0].astype(BF16), preferred_element_type=F32) + b_ref[0]


def _ada(c, w, b):
    nl, d, n = w.shape
    nb = c.shape[0]
    tn = min(TILES["ada_cols"], n)
    return pl.pallas_call(
        _ada_kernel,
        grid=(nl, n // tn),
        in_specs=[pl.BlockSpec((nb, d), lambda l, j: (0, 0)),
                  pl.BlockSpec((1, d, tn), lambda l, j: (l, 0, j)),
                  pl.BlockSpec((1, 1, tn), lambda l, j: (l, 0, j))],
        out_specs=pl.BlockSpec((1, nb, tn), lambda l, j: (l, 0, j)),
        out_shape=jax.ShapeDtypeStruct((nl, nb, n), F32),
        compiler_params=_cparams(2),
        name="ada_modulation",
    )(c, w, b.reshape(nl, 1, n))


def _norm_kernel(x_ref, g_ref, sc_ref, sh_ref, o_ref):
    _norm_outputs(x_ref[...], (g_ref, sc_ref, sh_ref), (o_ref,))


def _vec_spec(d, rows_per_batch_tile):
    return pl.BlockSpec((1, 1, d), lambda i, *_: (i // rows_per_batch_tile, 0, 0))


def _first_norm(x, g, sc, sh, seq):
    t, d = x.shape
    tm = min(TILES["norm_rows"], seq)
    tpb = seq // tm
    return pl.pallas_call(
        _norm_kernel,
        grid=(t // tm,),
        in_specs=[pl.BlockSpec((tm, d), lambda i: (i, 0)),
                  pl.BlockSpec((1, d), lambda i: (0, 0)),
                  _vec_spec(d, tpb), _vec_spec(d, tpb)],
        out_specs=pl.BlockSpec((tm, d), lambda i: (i, 0)),
        out_shape=jax.ShapeDtypeStruct((t, d), BF16),
        compiler_params=_cparams(1),
        name="first_norm",
    )(x, g, sc, sh)


def _proj_kernel(a_ref, w_ref, b_ref, o_ref, *, scale):
    y = jnp.dot(a_ref[...], w_ref[...], preferred_element_type=F32) + b_ref[...]
    if scale != 1.0:
        y = y * scale
    o_ref[...] = y.astype(o_ref.dtype)


def _proj(a, w, b, out_dtype, scale=1.0, name="proj"):
    t, d = a.shape
    n = w.shape[1]
    tm = min(TILES["mm_rows"], t)
    tn = min(TILES["mm_cols"], n)
    return pl.pallas_call(
        functools.partial(_proj_kernel, scale=scale),
        grid=(t // tm, n // tn),
        in_specs=[pl.BlockSpec((tm, d), lambda i, j: (i, 0)),
                  pl.BlockSpec((d, tn), lambda i, j: (0, j)),
                  pl.BlockSpec((1, tn), lambda i, j: (0, j))],
        out_specs=pl.BlockSpec((tm, tn), lambda i, j: (i, j)),
        out_shape=jax.ShapeDtypeStruct((t, n), out_dtype),
        compiler_params=_cparams(2),
        name=name,
    )(a, w, b)


def _glu_kernel(a_ref, wa_ref, wb_ref, ba_ref, bb_ref, o_ref):
    a = a_ref[...]
    lin = jnp.dot(a, wa_ref[...], preferred_element_type=F32) + ba_ref[...]
    gate = jnp.dot(a, wb_ref[...], preferred_element_type=F32) + bb_ref[...]
    o_ref[...] = (lin * _sigmoid(gate)).astype(o_ref.dtype)


def _glu_proj(a, w, b):
    t, d = a.shape
    tm = min(TILES["mm_rows"], t)
    tn = min(TILES["mm_cols"], d)
    nj = d // tn
    return pl.pallas_call(
        _glu_kernel,
        grid=(t // tm, nj),
        in_specs=[pl.BlockSpec((tm, d), lambda i, j: (i, 0)),
                  pl.BlockSpec((d, tn), lambda i, j: (0, j)),
                  pl.BlockSpec((d, tn), lambda i, j: (0, j + nj)),
                  pl.BlockSpec((1, tn), lambda i, j: (0, j)),
                  pl.BlockSpec((1, tn), lambda i, j: (0, j + nj))],
        out_specs=pl.BlockSpec((tm, tn), lambda i, j: (i, j)),
        out_shape=jax.ShapeDtypeStruct((t, d), BF16),
        compiler_params=_cparams(2),
        name="conv_pw1_glu",
    )(a, w, w, b, b)


def _causal_conv_ln_swish(u_ref, halo_ref, dw_ref, dwb_ref, lng_ref, lnb_ref,
                          win_ref, conv_ref, first_in_seq, width):
    tm, d = u_ref.shape
    halo = halo_ref[...].astype(F32)
    win_ref[0:HALO_ROWS, :] = jnp.where(first_in_seq, 0.0, halo)
    win_ref[HALO_ROWS:, :] = u_ref[...].astype(F32)
    base = HALO_ROWS - (width - 1)
    lane_chunk = min(4 * LANES, d)
    for c0 in range(0, d, lane_chunk):
        cols = slice(c0, c0 + lane_chunk)
        acc = jnp.zeros((tm, lane_chunk), F32) + dwb_ref[:, cols]
        for k in range(width):
            acc = acc + win_ref[base + k:base + k + tm, cols] * dw_ref[k:k + 1, cols]
        conv_ref[:, cols] = acc
    v = conv_ref[...]
    mu = jnp.mean(v, axis=-1, keepdims=True)
    vc = v - mu
    var = jnp.mean(vc * vc, axis=-1, keepdims=True)
    y = vc * lax.rsqrt(var + EPS) * lng_ref[...] + lnb_ref[...]
    return _silu(y).astype(BF16)


def _out_proj_kernel(*refs, conv_width, n_norms, emit_x, rows_per_seq_tile):
    it = iter(refs)
    if conv_width:
        u_ref, halo_ref, dw_ref, dwb_ref, lng_ref, lnb_ref = (next(it) for _ in range(6))
    else:
        a_ref = next(it)
    w_ref, b_ref, x_ref, gate_ref = (next(it) for _ in range(4))
    norm_refs = [next(it) for _ in range(3 * n_norms)]
    xo_ref = next(it) if emit_x else None
    out_refs = [next(it) for _ in range(n_norms)]
    if conv_width:
        win_ref, conv_ref = next(it), next(it)
        first = (pl.program_id(0) % rows_per_seq_tile) == 0
        a = _causal_conv_ln_swish(u_ref, halo_ref, dw_ref, dwb_ref, lng_ref, lnb_ref,
                                  win_ref, conv_ref, first, conv_width)
    else:
        a = a_ref[...]
    y = jnp.dot(a, w_ref[...], preferred_element_type=F32) + b_ref[...]
    xn = x_ref[...] + gate_ref[0] * y
    if emit_x:
        xo_ref[...] = xn
    _norm_outputs(xn, norm_refs, out_refs)


def _out_proj(a, w, b, x, gate, norms, out_dtypes, seq, conv=None, emit_x=True, name="out_proj"):
    t, d = x.shape
    tm = min(TILES["res_rows"], seq)
    tpb = seq // tm
    row = pl.BlockSpec((tm, d), lambda i: (i, 0))
    full = lambda r, c: pl.BlockSpec((r, c), lambda i: (0, 0))
    in_specs, args, scratch = [], [], []
    width = 0
    if conv is not None:
        dw, dwb, lng, lnb = conv
        width = dw.shape[0]
        assert width - 1 <= HALO_ROWS and tm % HALO_ROWS == 0
        hpt = tm // HALO_ROWS
        in_specs += [row,
                     pl.BlockSpec((HALO_ROWS, d), lambda i: (jnp.maximum(i * hpt - 1, 0), 0)),
                     full(width, d), full(1, d), full(1, d), full(1, d)]
        args += [a, a, dw, dwb, lng, lnb]
        scratch = [pltpu.VMEM((tm + HALO_ROWS, d), F32), pltpu.VMEM((tm, d), F32)]
    else:
        in_specs += [row]
        args += [a]
    in_specs += [full(d, d), full(1, d), row, _vec_spec(d, tpb)]
    args += [w, b, x, gate]
    for g, sc, sh in norms:
        in_specs += [full(1, d), _vec_spec(d, tpb), _vec_spec(d, tpb)]
        args += [g, sc, sh]
    out_specs = [row] * (len(norms) + int(emit_x))
    out_shape = ([jax.ShapeDtypeStruct((t, d), F32)] if emit_x else []) + \
        [jax.ShapeDtypeStruct((t, d), dt) for dt in out_dtypes]
    return pl.pallas_call(
        functools.partial(_out_proj_kernel, conv_width=width, n_norms=len(norms),
                          emit_x=emit_x, rows_per_seq_tile=tpb),
        grid=(t // tm,),
        in_specs=in_specs, out_specs=out_specs, out_shape=out_shape,
        scratch_shapes=scratch,
        compiler_params=_cparams(1),
        name=name,
    )(*args)


def _cum_kernel(fl_ref, bf_ref, o_ref, *, blk):
    s = fl_ref.shape[0]
    r = lax.broadcasted_iota(I32, (blk, blk), 0)
    c = lax.broadcasted_iota(I32, (blk, blk), 1)
    tri = (r >= c).astype(BF16)
    carry = jnp.zeros((1, fl_ref.shape[1]), F32)
    for b0 in range(0, s, blk):
        z = fl_ref[b0:b0 + blk, :] + bf_ref[...]
        lf = jnp.minimum(z, 0.0) - jnp.log(1.0 + jnp.exp(-jnp.abs(z)))
        p0 = lf.astype(BF16)
        r1 = lf - p0.astype(F32)
        p1 = r1.astype(BF16)
        p2 = (r1 - p1.astype(F32)).astype(BF16)
        cs = (jnp.dot(tri, p0, preferred_element_type=F32)
              + jnp.dot(tri, p1, preferred_element_type=F32)
              + jnp.dot(tri, p2, preferred_element_type=F32)) + carry
        o_ref[b0:b0 + blk, :] = cs
        carry = cs[blk - 1:blk, :]


def _forget_cumsum(fl, bf, seq):
    t, n = fl.shape
    blk = min(TILES["cum_rows"], seq)
    return pl.pallas_call(
        functools.partial(_cum_kernel, blk=blk),
        grid=(t // seq,),
        in_specs=[pl.BlockSpec((seq, n), lambda b: (b, 0)),
                  pl.BlockSpec((1, n), lambda b: (0, 0))],
        out_specs=pl.BlockSpec((seq, n), lambda b: (b, 0)),
        out_shape=jax.ShapeDtypeStruct((t, n), F32),
        compiler_params=_cparams(1),
        name="forget_cumsum",
    )(fl, bf)


def _attn_kernel(q_ref, k_ref, v_ref, fs_ref, ft_ref, o_ref, *, tq):
    i = pl.program_id(2)
    q = q_ref[...]
    ft = ft_ref[0]
    dh = q.shape[1]

    def block(j, carry, diagonal):
        m, l, acc = carry
        k0 = pl.multiple_of(j * tq, tq)
        kb = k_ref[pl.ds(k0, tq), :]
        vb = v_ref[pl.ds(k0, tq), :]
        s = lax.dot_general(q, kb, (((1,), (1,)), ((), ())), preferred_element_type=F32)
        s = s + ft - fs_ref[0, j]
        if diagonal:
            r = lax.broadcasted_iota(I32, (tq, tq), 0)
            c = lax.broadcasted_iota(I32, (tq, tq), 1)
            s = jnp.where(r >= c, s, -jnp.inf)
        m_new = jnp.maximum(m, jnp.max(s, axis=-1, keepdims=True))
        p = jnp.exp(s - m_new)
        alpha = jnp.exp(m - m_new)
        l = alpha * l + jnp.sum(p, axis=-1, keepdims=True)
        acc = alpha * acc + jnp.dot(p.astype(BF16), vb, preferred_element_type=F32)
        return m_new, l, acc

    init = (jnp.full((tq, 1), -jnp.inf, F32), jnp.zeros((tq, 1), F32), jnp.zeros((tq, dh), F32))
    carry = lax.fori_loop(0, i, lambda j, c: block(j, c, False), init)
    _, l, acc = block(i, carry, True)
    o_ref[...] = (acc / l).astype(o_ref.dtype)


def _attention(q, kv, cum_rows, cum_cols, nb, seq, nh):
    t, d = q.shape
    dh = d // nh
    tq = min(TILES["attn_q"], seq)
    nq = seq // tq
    return pl.pallas_call(
        functools.partial(_attn_kernel, tq=tq),
        grid=(nb, nh, nq),
        in_specs=[pl.BlockSpec((tq, dh), lambda b, h, i: (b * nq + i, h)),
                  pl.BlockSpec((seq, dh), lambda b, h, i: (b, h)),
                  pl.BlockSpec((seq, dh), lambda b, h, i: (b, nh + h)),
                  pl.BlockSpec((1, nq, 1, tq), lambda b, h, i: (b * nh + h, 0, 0, 0)),
                  pl.BlockSpec((1, tq, 1), lambda b, h, i: (b * nh + h, i, 0))],
        out_specs=pl.BlockSpec((tq, dh), lambda b, h, i: (b * nq + i, h)),
        out_shape=jax.ShapeDtypeStruct((t, d), BF16),
        compiler_params=_cparams(3),
        name="forgetting_attention",
    )(q, kv, kv, cum_rows.reshape(nb * nh, nq, 1, tq), cum_cols)


def _ffn_kernel(te_ref, nv_ref, *refs, grouped, n_norms, emit_x, cast_w, chunk):
    it = iter(refs)
    h_ref, wg_ref, wu_ref, wd_ref = (next(it) for _ in range(4))
    if not grouped:
        x_ref, gate_ref = next(it), next(it)
        norm_refs = [next(it) for _ in range(3 * n_norms)]
        xo_ref = next(it) if emit_x else None
        out_refs = [next(it) for _ in range(n_norms)]
    else:
        y_ref = next(it)
    acc_ref = next(it)
    hb_ref = next(it) if grouped else h_ref
    i, f = pl.program_id(0), pl.program_id(1)
    nf = pl.num_programs(1)
    tm = acc_ref.shape[0]

    @pl.when(f == 0)
    def _():
        acc_ref[...] = jnp.zeros_like(acc_ref)
        if grouped:
            hi, lo = _unpack_halves(h_ref[...])
            hb_ref[...] = jnp.concatenate([hi, lo], axis=1).astype(BF16)

    def step():
        h = hb_ref[...]
        wg, wu, wd = wg_ref[0], wu_ref[0], wd_ref[0]
        if cast_w:
            wg, wu, wd = wg.astype(BF16), wu.astype(BF16), wd.astype(BF16)
        g = jnp.dot(h, wg, preferred_element_type=F32)
        u = jnp.dot(h, wu, preferred_element_type=F32)
        a = (_silu(g) * u).astype(BF16)
        acc_ref[...] += jnp.dot(a, wd, preferred_element_type=F32)

    if grouped:
        pl.when(nv_ref[i] > 0)(step)
    else:
        step()

    @pl.when(f == nf - 1)
    def _():
        if grouped:
            y_ref[...] = _pack_halves(acc_ref[...])
        else:
            def rows_chunk(c, _):
                r0 = pl.multiple_of(c * chunk, chunk)
                rs = pl.ds(r0, chunk)
                xn = x_ref[rs, :] + gate_ref[0] * acc_ref[rs, :]
                if emit_x:
                    xo_ref[rs, :] = xn
                _norm_outputs(xn, norm_refs, [o.at[rs, :] for o in out_refs])
                return 0
            lax.fori_loop(0, tm // chunk, rows_chunk, 0)


def _ffn_weight_specs(d, ff, tf, nf):
    def col(i, f, te, nv):
        return (te[i], 0, jnp.where(nv[i] > 0, f, nf - 1))

    def row(i, f, te, nv):
        return (te[i], jnp.where(nv[i] > 0, f, nf - 1), 0)
    return [pl.BlockSpec((1, d, tf), col), pl.BlockSpec((1, d, tf), col), pl.BlockSpec((1, tf, d), row)]


def _dense_ffn(h, wg, wu, wd, x, gate, norms, out_dtypes, seq, emit_x=True):
    t, d = x.shape
    ff = wg.shape[-1]
    tm = min(TILES["ffn_rows"], seq)
    tf = min(TILES["ffn_cols"], ff)
    nf = ff // tf
    tpb = seq // tm
    nt = t // tm
    row = pl.BlockSpec((tm, d), lambda i, f, te, nv: (i, 0))
    full = pl.BlockSpec((1, d), lambda i, f, te, nv: (0, 0))
    vec = pl.BlockSpec((1, 1, d), lambda i, f, te, nv: (i // tpb, 0, 0))
    in_specs = [row] + _ffn_weight_specs(d, ff, tf, nf) + [row, vec]
    args = [h, wg, wu, wd, x, gate]
    for g, sc, sh in norms:
        in_specs += [full, vec, vec]
        args += [g, sc, sh]
    out_specs = [row] * (len(norms) + int(emit_x))
    out_shape = ([jax.ShapeDtypeStruct((t, d), F32)] if emit_x else []) + \
        [jax.ShapeDtypeStruct((t, d), dt) for dt in out_dtypes]
    chunk = min(TILES["epi_chunk"], tm)
    return pl.pallas_call(
        functools.partial(_ffn_kernel, grouped=False, n_norms=len(norms), emit_x=emit_x,
                          cast_w=wg.dtype != BF16, chunk=chunk),
        grid_spec=pltpu.PrefetchScalarGridSpec(
            num_scalar_prefetch=2, grid=(nt, nf),
            in_specs=in_specs, out_specs=out_specs,
            scratch_shapes=[pltpu.VMEM((tm, d), F32)]),
        out_shape=out_shape,
        compiler_params=_cparams(2),
        name="dense_swiglu",
    )(jnp.zeros((nt,), I32), jnp.ones((nt,), I32), *args)


def _grouped_ffn(hs, wg, wu, wd, tile_expert, tile_rows, tm):
    r, dp = hs.shape
    d = 2 * dp
    ff = wg.shape[-1]
    tf = min(TILES["moe_cols"], ff)
    nf = ff // tf
    row = pl.BlockSpec((tm, dp), lambda i, f, te, nv: (i, 0))
    return pl.pallas_call(
        functools.partial(_ffn_kernel, grouped=True, n_norms=0, emit_x=False,
                          cast_w=wg.dtype != BF16, chunk=0),
        grid_spec=pltpu.PrefetchScalarGridSpec(
            num_scalar_prefetch=2, grid=(r // tm, nf),
            in_specs=[row] + _ffn_weight_specs(d, ff, tf, nf),
            out_specs=row,
            scratch_shapes=[pltpu.VMEM((tm, d), F32), pltpu.VMEM((tm, d), BF16)]),
        out_shape=jax.ShapeDtypeStruct((r, dp), U32),
        compiler_params=_cparams(2),
        name="expert_swiglu",
    )(tile_expert, tile_rows, hs, wg, wu, wd)


def _route_kernel(lg_ref, info_ref, cnt_ref, carry_ref, *, n_experts):
    tr, n = lg_ref.shape

    @pl.when(pl.program_id(0) == 0)
    def _():
        carry_ref[...] = jnp.zeros_like(carry_ref)

    lane = lax.broadcasted_iota(I32, (tr, n), 1).astype(F32)
    lg = jnp.where(lane < n_experts, lg_ref[...], -jnp.inf)
    v1 = jnp.max(lg, axis=-1, keepdims=True)
    i1 = jnp.min(jnp.where(lg == v1, lane, float(n)), axis=-1, keepdims=True)
    lg2 = jnp.where(lane == i1, -jnp.inf, lg)
    v2 = jnp.max(lg2, axis=-1, keepdims=True)
    i2 = jnp.min(jnp.where(lg2 == v2, lane, float(n)), axis=-1, keepdims=True)
    e = jnp.exp(v2 - v1)
    w1 = 1.0 / (1.0 + e)
    w2 = e / (1.0 + e)
    hot1 = lane == i1
    hot2 = lane == i2
    hot = jnp.where(hot1 | hot2, 1.0, 0.0)
    r = lax.broadcasted_iota(I32, (tr, tr), 0)
    c = lax.broadcasted_iota(I32, (tr, tr), 1)
    before = (r > c).astype(BF16)
    seen = jnp.dot(before, hot.astype(BF16), preferred_element_type=F32) + carry_ref[...]
    r1 = jnp.sum(jnp.where(hot1, seen, 0.0), axis=-1, keepdims=True)
    r2 = jnp.sum(jnp.where(hot2, seen, 0.0), axis=-1, keepdims=True)
    carry_ref[...] += jnp.sum(hot, axis=0, keepdims=True)
    cnt_ref[...] = carry_ref[...]
    cols = (i1, i2, w1, w2, r1, r2)
    out = jnp.zeros((tr, n), F32)
    for k, col in enumerate(cols):
        out = jnp.where(lane == float(k), col, out)
    info_ref[...] = out


def _route(logits, n_experts):
    t, n = logits.shape
    tr = min(TILES["route_rows"], t)
    return pl.pallas_call(
        functools.partial(_route_kernel, n_experts=n_experts),
        grid=(t // tr,),
        in_specs=[pl.BlockSpec((tr, n), lambda i: (i, 0))],
        out_specs=[pl.BlockSpec((tr, n), lambda i: (i, 0)), pl.BlockSpec((1, n), lambda i: (0, 0))],
        out_shape=[jax.ShapeDtypeStruct((t, n), F32), jax.ShapeDtypeStruct((1, n), F32)],
        scratch_shapes=[pltpu.VMEM((1, n), F32)],
        compiler_params=_cparams(1),
        name="route_top2",
    )(logits)


def _dispatch_kernel(pos_ref, tile_rows_ref, h_ref, hs_ref, buf_ref, zero_ref, sem, zsem):
    tr = h_ref.shape[0]
    base = pl.program_id(0) * (TOP_K * tr)

    @pl.when(pl.program_id(0) == 0)
    def _():
        zero_ref[...] = jnp.zeros_like(zero_ref)
        tm = zero_ref.shape[0]

        def fill(i):
            return pltpu.make_async_copy(zero_ref, hs_ref.at[pl.ds(i * tm, tm)], zsem)

        for i in range(tile_rows_ref.shape[0]):
            pl.when(tile_rows_ref[i] < tm)(lambda i=i: fill(i).start())
        for i in range(tile_rows_ref.shape[0]):
            pl.when(tile_rows_ref[i] < tm)(lambda i=i: fill(i).wait())

    buf_ref[...] = _pack_halves(h_ref[...].astype(F32))

    def row_copy(t, k):
        return pltpu.make_async_copy(buf_ref.at[pl.ds(t, 1)],
                                     hs_ref.at[pl.ds(pos_ref[base + TOP_K * t + k], 1)], sem)

    def start(t, _):
        for k in range(TOP_K):
            row_copy(t, k).start()
        return 0

    def wait(t, _):
        for k in range(TOP_K):
            row_copy(t, k).wait()
        return 0

    lax.fori_loop(0, tr, start, 0)
    lax.fori_loop(0, tr, wait, 0)


def _dispatch(h, pos, tile_rows, tm):
    n_rows = tile_rows.shape[0] * tm
    t, d = h.shape
    tr = min(TILES["dispatch_rows"], t)
    return pl.pallas_call(
        _dispatch_kernel,
        grid_spec=pltpu.PrefetchScalarGridSpec(
            num_scalar_prefetch=2, grid=(t // tr,),
            in_specs=[pl.BlockSpec((tr, d), lambda i, pos, lt: (i, 0))],
            out_specs=pl.BlockSpec(memory_space=pl.ANY),
            scratch_shapes=[pltpu.VMEM((tr, d // 2), U32), pltpu.VMEM((tm, d // 2), U32),
                            pltpu.SemaphoreType.DMA, pltpu.SemaphoreType.DMA]),
        out_shape=jax.ShapeDtypeStruct((n_rows, d // 2), U32),
        compiler_params=_cparams(1),
        name="moe_dispatch",
    )(pos, tile_rows, h)


def _combine_kernel(pos_ref, *refs, n_norms, emit_x):
    it = iter(refs)
    info_ref, x_ref, gate_ref = (next(it) for _ in range(3))
    norm_refs = [next(it) for _ in range(3 * n_norms)]
    ys_ref = next(it)
    xo_ref = next(it) if emit_x else None
    out_refs = [next(it) for _ in range(n_norms)]
    buf_ref, sem = next(it), next(it)
    tc = x_ref.shape[0]
    base = pl.program_id(0) * (TOP_K * tc)

    def row_copy(t, k):
        return pltpu.make_async_copy(ys_ref.at[pl.ds(pos_ref[base + TOP_K * t + k], 1)],
                                     buf_ref.at[k, pl.ds(t, 1)], sem)

    def start(t, _):
        for k in range(TOP_K):
            row_copy(t, k).start()
        return 0

    def wait(t, _):
        for k in range(TOP_K):
            row_copy(t, k).wait()
        return 0

    lax.fori_loop(0, tc, start, 0)
    lax.fori_loop(0, tc, wait, 0)
    info = info_ref[...]
    w1, w2 = info[:, 2:3], info[:, 3:4]
    a_hi, a_lo = _unpack_halves(buf_ref[0])
    b_hi, b_lo = _unpack_halves(buf_ref[1])
    y = jnp.concatenate([w1 * a_hi + w2 * b_hi, w1 * a_lo + w2 * b_lo], axis=1)
    xn = x_ref[...] + gate_ref[0] * y
    if emit_x:
        xo_ref[...] = xn
    _norm_outputs(xn, norm_refs, out_refs)


def _combine(ys, pos, info, x, gate, norms, out_dtypes, seq, emit_x=True):
    t, d = x.shape
    tc = min(TILES["combine_rows"], seq)
    tpb = seq // tc
    row = pl.BlockSpec((tc, d), lambda i, pos: (i, 0))
    full = pl.BlockSpec((1, d), lambda i, pos: (0, 0))
    vec = pl.BlockSpec((1, 1, d), lambda i, pos: (i // tpb, 0, 0))
    in_specs = [pl.BlockSpec((tc, info.shape[1]), lambda i, pos: (i, 0)), row, vec]
    args = [info, x, gate]
    for g, sc, sh in norms:
        in_specs += [full, vec, vec]
        args += [g, sc, sh]
    in_specs += [pl.BlockSpec(memory_space=pl.ANY)]
    args += [ys]
    out_specs = [row] * (len(norms) + int(emit_x))
    out_shape = ([jax.ShapeDtypeStruct((t, d), F32)] if emit_x else []) + \
        [jax.ShapeDtypeStruct((t, d), dt) for dt in out_dtypes]
    return pl.pallas_call(
        functools.partial(_combine_kernel, n_norms=len(norms), emit_x=emit_x),
        grid_spec=pltpu.PrefetchScalarGridSpec(
            num_scalar_prefetch=1, grid=(t // tc,),
            in_specs=in_specs, out_specs=out_specs,
            scratch_shapes=[pltpu.VMEM((TOP_K, tc, d // 2), U32), pltpu.SemaphoreType.DMA]),
        out_shape=out_shape,
        compiler_params=_cparams(1),
        name="moe_combine",
    )(pos, *args)


def _moe_layer(h, x, gate, norms, out_dtypes, seq, router_w, router_b, wg, wu, wd, emit_x=True):
    t, d = h.shape
    ne = router_w.shape[1]
    tm = min(TILES["moe_rows"], t)
    rw = jnp.zeros((d, LANES), BF16).at[:, :ne].set(router_w.astype(BF16))
    rb = jnp.zeros((1, LANES), F32).at[0, :ne].set(router_b)
    logits = _proj(h, rw, rb, F32, name="router_logits")
    info, counts = _route(logits, ne)
    counts = counts[0, :ne].astype(I32)
    padded = (counts + tm - 1) // tm * tm
    ends = jnp.cumsum(padded)
    starts = ends - padded
    n_tiles = (TOP_K * t) // tm + ne
    tile_row0 = jnp.arange(n_tiles, dtype=I32) * tm
    tile_e = jnp.minimum(jnp.sum(tile_row0[:, None] >= ends[None, :], axis=1), ne - 1).astype(I32)
    tile_rows = jnp.clip(starts[tile_e] + counts[tile_e] - tile_row0, 0, tm)
    tile_rows = jnp.where(tile_row0 < ends[-1], tile_rows, 0).astype(I32)
    last_e = jnp.max(jnp.where(counts > 0, jnp.arange(ne, dtype=I32), 0))
    tile_e = jnp.where(tile_row0 < ends[-1], tile_e, last_e).astype(I32)
    e_idx = info[:, 0:TOP_K].astype(I32)
    rank = info[:, 4:4 + TOP_K].astype(I32)
    start_of = jnp.sum(jnp.where(e_idx[:, :, None] == jnp.arange(ne, dtype=I32), starts, 0), axis=-1)
    pos = (start_of + rank).reshape(-1).astype(I32)
    hs = _dispatch(h, pos, tile_rows, tm)
    ys = _grouped_ffn(hs, wg, wu, wd, tile_e, tile_rows, tm)
    return _combine(ys, pos, info, x, gate, norms, out_dtypes, seq, emit_x=emit_x)


def kernel(x, c, ada_w, ada_b, norm_mix_g, norm_ffn_g, conv_pw1_w, conv_pw1_b, conv_dw_w, conv_dw_b, conv_ln_g, conv_ln_b, conv_pw2_w, conv_pw2_b, kv_ada_w, kv_ada_b, kv_norm_g, w_kvf, b_f, attn_wq, attn_wo, ffn_w_gate, ffn_w_up, ffn_w_down, moe_router_w, moe_router_b, moe_w_gate, moe_w_up, moe_w_down, final_norm_g):
    nb, seq, d = x.shape
    depth = ada_w.shape[0]
    n_conv = conv_pw1_w.shape[0]
    nh = b_f.shape[0]
    t = nb * seq
    xs = x.reshape(t, d)

    mods = _ada(c, ada_w, ada_b)
    kv_mod = _ada(c, kv_ada_w[None], kv_ada_b[None])[0]

    def vec(v):
        return v.reshape(nb, 1, d)

    def mod(layer, k):
        return vec(mods[layer, :, k * d:(k + 1) * d])

    row = lambda v: v.reshape(1, d)
    zero_vec = jnp.zeros((nb, 1, d), F32)

    def mix_norm(layer):
        return (row(norm_mix_g[layer]), mod(layer, 1), mod(layer, 0))

    def ffn_norm(layer):
        return (row(norm_ffn_g[layer]), mod(layer, 4), mod(layer, 3))

    def after_ffn(layer):
        if layer == depth - 1:
            return [(row(final_norm_g), zero_vec, zero_vec)], [F32]
        norms, dts = [mix_norm(layer + 1)], [BF16]
        if layer + 1 == n_conv:
            norms.append((row(kv_norm_g), vec(kv_mod[:, d:]), vec(kv_mod[:, :d])))
            dts.append(BF16)
        return norms, dts

    h = _first_norm(xs, *mix_norm(0), seq)
    kv = cum_rows = cum_cols = None
    out = None
    for layer in range(depth):
        g1, g2 = mod(layer, 2), mod(layer, 5)
        if layer < n_conv:
            i = layer
            u = _glu_proj(h, conv_pw1_w[i].astype(BF16), conv_pw1_b[i].reshape(1, 2 * d))
            xs, h = _out_proj(u, conv_pw2_w[i].astype(BF16), row(conv_pw2_b[i]), xs, g1,
                              [ffn_norm(layer)], [BF16], seq,
                              conv=(conv_dw_w[i], row(conv_dw_b[i]), row(conv_ln_g[i]), row(conv_ln_b[i])),
                              name="conv_out_proj")
        else:
            i = layer - n_conv
            q = _proj(h, attn_wq[i].astype(BF16), jnp.zeros((1, d), F32), BF16,
                      scale=float(d // nh) ** -0.5, name="attn_q_proj")
            o = _attention(q, kv, cum_rows, cum_cols, nb, seq, nh)
            xs, h = _out_proj(o, attn_wo[i].astype(BF16), jnp.zeros((1, d), F32), xs, g1,
                              [ffn_norm(layer)], [BF16], seq, name="attn_out_proj")
        norms, dts = after_ffn(layer)
        emit_x = layer != depth - 1
        j = layer // 2
        if layer % 2 == 0:
            outs = _dense_ffn(h, ffn_w_gate[j].astype(BF16)[None], ffn_w_up[j].astype(BF16)[None],
                              ffn_w_down[j].astype(BF16)[None], xs, g2, norms, dts, seq, emit_x=emit_x)
        else:
            outs = _moe_layer(h, xs, g2, norms, dts, seq, moe_router_w[j], moe_router_b[j],
                              moe_w_gate[j], moe_w_up[j], moe_w_down[j], emit_x=emit_x)
        if emit_x:
            xs, h = outs[0], outs[1]
        else:
            out = outs[0]
        if layer + 1 == n_conv:
            h_kv = outs[2]
            kv = _proj(h_kv, w_kvf[:, :2 * d].astype(BF16), jnp.zeros((1, 2 * d), F32), BF16,
                       name="shared_kv_proj")
            wf = jnp.zeros((d, LANES), BF16).at[:, :nh].set(w_kvf[:, 2 * d:].astype(BF16))
            fl = _proj(h_kv, wf, jnp.zeros((1, LANES), F32), F32, name="forget_logits")
            bf = jnp.zeros((1, LANES), F32).at[0, :nh].set(b_f)
            cum = _forget_cumsum(fl, bf, seq)[:, :nh].reshape(nb, seq, nh)
            cum_rows = cum.transpose(0, 2, 1).reshape(nb * nh, seq)
            cum_cols = cum_rows.reshape(nb * nh, seq, 1)
    return out.reshape(nb, seq, d)
```

```python
import functools

import jax
import jax.numpy as jnp
from jax import lax
from jax.experimental import pallas as pl
from jax.experimental.pallas import tpu as pltpu

F32 = jnp.float32
BF16 = jnp.bfloat16
U32 = jnp.uint32
I32 = jnp.int32

EPS = 1e-6
TOP_K = 2
LANES = 128
MXU_DIM = 256
HALO_ROWS = 32
HI_MASK = 0xFFFF0000
SUBLANES = 8
LOG2E = 1.4426950408889634

TILES = dict(
    norm_rows=512,
    mm_rows=1024, mm_cols=512,
    res_rows=256,
    attn_q=512, attn_keys=512,
    conv_rows=32, conv_lanes=512,
    dma_unroll=8,
    ffn_rows=512, ffn_cols=512,
    moe_rows=1024, moe_cols=256,
    route_rows=512,
    dispatch_rows=512,
    combine_rows=256,
    ada_cols=1024,
    cum_rows=256,
    epi_chunk=128,
)
VMEM_LIMIT = 56 * 2**20


def _cparams(n_axes, vmem=VMEM_LIMIT):
    return pltpu.CompilerParams(dimension_semantics=("arbitrary",) * n_axes,
                                vmem_limit_bytes=vmem)


def _sigmoid(v):
    return 1.0 / (1.0 + jnp.exp(-v))


def _silu(v):
    return v * _sigmoid(v)


def _pack_halves(v):
    n = v.shape[1] // 2
    hi = lax.bitcast_convert_type(v[:, :n].astype(BF16).astype(F32), U32)
    lo = lax.bitcast_convert_type(v[:, n:].astype(BF16).astype(F32), U32)
    return hi | lax.shift_right_logical(lo, jnp.uint32(16))


def _unpack_halves(p):
    hi = lax.bitcast_convert_type(p & jnp.uint32(HI_MASK), F32)
    lo = lax.bitcast_convert_type(lax.shift_left(p, jnp.uint32(16)), F32)
    return hi, lo


def _norm_outputs(xn, norm_refs, out_refs):
    ms = jnp.mean(xn * xn, axis=-1, keepdims=True)
    xr = xn * lax.rsqrt(ms + EPS)
    for k, o_ref in enumerate(out_refs):
        g = norm_refs[3 * k][...]
        sc = norm_refs[3 * k + 1][0]
        sh = norm_refs[3 * k + 2][0]
        o_ref[...] = (xr * g * (1.0 + sc) + sh).astype(o_ref.dtype)


def _ada_kernel(c_ref, w_ref, b_ref, o_ref):
    ca = _silu(c_ref[...]).astype(BF16)
    o_ref[0] = jnp.dot(ca, w_ref[0].astype(BF16), preferred_element_type=F32) + b_ref[0]


def _ada(c, w, b):
    nl, d, n = w.shape
    nb = c.shape[0]
    tn = min(TILES["ada_cols"], n)
    return pl.pallas_call(
        _ada_kernel,
        grid=(nl, n // tn),
        in_specs=[pl.BlockSpec((nb, d), lambda l, j: (0, 0)),
                  pl.BlockSpec((1, d, tn), lambda l, j: (l, 0, j)),
                  pl.BlockSpec((1, 1, tn), lambda l, j: (l, 0, j))],
        out_specs=pl.BlockSpec((1, nb, tn), lambda l, j: (l, 0, j)),
        out_shape=jax.ShapeDtypeStruct((nl, nb, n), F32),
        compiler_params=_cparams(2),
        name="ada_modulation",
    )(c, w, b.reshape(nl, 1, n))


def _norm_kernel(x_ref, g_ref, sc_ref, sh_ref, o_ref):
    _norm_outputs(x_ref[...], (g_ref, sc_ref, sh_ref), (o_ref,))


def _vec_spec(d, rows_per_batch_tile):
    return pl.BlockSpec((1, 1, d), lambda i, *_: (i // rows_per_batch_tile, 0, 0))


def _first_norm(x, g, sc, sh, seq):
    t, d = x.shape
    tm = min(TILES["norm_rows"], seq)
    tpb = seq // tm
    return pl.pallas_call(
        _norm_kernel,
        grid=(t // tm,),
        in_specs=[pl.BlockSpec((tm, d), lambda i: (i, 0)),
                  pl.BlockSpec((1, d), lambda i: (0, 0)),
                  _vec_spec(d, tpb), _vec_spec(d, tpb)],
        out_specs=pl.BlockSpec((tm, d), lambda i: (i, 0)),
        out_shape=jax.ShapeDtypeStruct((t, d), BF16),
        compiler_params=_cparams(1),
        name="first_norm",
    )(x, g, sc, sh)


def _proj_kernel(a_ref, w_ref, b_ref, o_ref, *, scale):
    y = jnp.dot(a_ref[...], w_ref[...], preferred_element_type=F32) + b_ref[...]
    if scale != 1.0:
        y = y * scale
    o_ref[...] = y.astype(o_ref.dtype)


def _proj(a, w, b, out_dtype, scale=1.0, name="proj"):
    t, d = a.shape
    n = w.shape[1]
    tm = min(TILES["mm_rows"], t)
    tn = min(TILES["mm_cols"], n)
    return pl.pallas_call(
        functools.partial(_proj_kernel, scale=scale),
        grid=(t // tm, n // tn),
        in_specs=[pl.BlockSpec((tm, d), lambda i, j: (i, 0)),
                  pl.BlockSpec((d, tn), lambda i, j: (0, j)),
                  pl.BlockSpec((1, tn), lambda i, j: (0, j))],
        out_specs=pl.BlockSpec((tm, tn), lambda i, j: (i, j)),
        out_shape=jax.ShapeDtypeStruct((t, n), out_dtype),
        compiler_params=_cparams(2),
        name=name,
    )(a, w, b)


def _glu_kernel(a_ref, wa_ref, wb_ref, ba_ref, bb_ref, o_ref):
    a = a_ref[...]
    lin = jnp.dot(a, wa_ref[...], preferred_element_type=F32) + ba_ref[...]
    gate = jnp.dot(a, wb_ref[...], preferred_element_type=F32) + bb_ref[...]
    o_ref[...] = (lin * _sigmoid(gate)).astype(o_ref.dtype)


def _glu_proj(a, w, b):
    t, d = a.shape
    tm = min(TILES["mm_rows"], t)
    tn = min(TILES["mm_cols"], d)
    nj = d // tn
    return pl.pallas_call(
        _glu_kernel,
        grid=(t // tm, nj),
        in_specs=[pl.BlockSpec((tm, d), lambda i, j: (i, 0)),
                  pl.BlockSpec((d, tn), lambda i, j: (0, j)),
                  pl.BlockSpec((d, tn), lambda i, j: (0, j + nj)),
                  pl.BlockSpec((1, tn), lambda i, j: (0, j)),
                  pl.BlockSpec((1, tn), lambda i, j: (0, j + nj))],
        out_specs=pl.BlockSpec((tm, tn), lambda i, j: (i, j)),
        out_shape=jax.ShapeDtypeStruct((t, d), BF16),
        compiler_params=_cparams(2),
        name="conv_pw1_glu",
    )(a, w, w, b, b)


def _causal_conv_ln_swish(u_ref, halo_ref, dw_ref, dwb_ref, lng_ref, lnb_ref,
                          win_ref, shift_ref, taps_ref, conv_ref, first_in_seq, width):
    tm, d = u_ref.shape
    n = tm + HALO_ROWS
    halo = halo_ref[...].astype(F32)
    win_ref[0:HALO_ROWS, :] = jnp.where(first_in_seq, 0.0, halo)
    win_ref[HALO_ROWS:, :] = u_ref[...].astype(F32)
    base = HALO_ROWS - (width - 1)
    rc = min(TILES["conv_rows"], tm)
    lc = shift_ref.shape[2]
    for c0 in range(0, d, lc):
        cols = slice(c0, c0 + lc)
        for b in range(1, SUBLANES):
            shift_ref[b - 1, 0:n - SUBLANES, :] = win_ref[b:n - SUBLANES + b, cols]

        for k in range(width):
            taps_ref[k] = jnp.broadcast_to(dw_ref[k:k + 1, cols], (SUBLANES, lc))
        taps_ref[width] = jnp.broadcast_to(dwb_ref[:, cols], (SUBLANES, lc))

        def rows_chunk(ci, _, cols=cols):
            r0 = ci * rc
            groups = range(rc // SUBLANES)
            acc = [taps_ref[width] for _ in groups]
            for k in range(width):
                blk, b = divmod(base + k, SUBLANES)
                w = taps_ref[k]
                for g in groups:
                    rs = pl.ds(pl.multiple_of(r0 + (blk + g) * SUBLANES, SUBLANES), SUBLANES)
                    src = win_ref[rs, cols] if b == 0 else shift_ref[b - 1, rs, :]
                    acc[g] = acc[g] + src * w
            for g in groups:
                conv_ref[pl.ds(pl.multiple_of(r0 + g * SUBLANES, SUBLANES), SUBLANES), cols] = acc[g]
            return 0

        lax.fori_loop(0, tm // rc, rows_chunk, 0)
    v = conv_ref[...]
    mu = jnp.mean(v, axis=-1, keepdims=True)
    vc = v - mu
    var = jnp.mean(vc * vc, axis=-1, keepdims=True)
    y = vc * lax.rsqrt(var + EPS) * lng_ref[...] + lnb_ref[...]
    return _silu(y).astype(BF16)


def _out_proj_kernel(*refs, conv_width, n_norms, emit_x, rows_per_seq_tile):
    it = iter(refs)
    if conv_width:
        u_ref, halo_ref, dw_ref, dwb_ref, lng_ref, lnb_ref = (next(it) for _ in range(6))
    else:
        a_ref = next(it)
    w_ref, b_ref, x_ref, gate_ref = (next(it) for _ in range(4))
    norm_refs = [next(it) for _ in range(3 * n_norms)]
    xo_ref = next(it) if emit_x else None
    out_refs = [next(it) for _ in range(n_norms)]
    if conv_width:
        win_ref, shift_ref, taps_ref, conv_ref = (next(it) for _ in range(4))
        first = (pl.program_id(0) % rows_per_seq_tile) == 0
        a = _causal_conv_ln_swish(u_ref, halo_ref, dw_ref, dwb_ref, lng_ref, lnb_ref,
                                  win_ref, shift_ref, taps_ref, conv_ref, first, conv_width)
    else:
        a = a_ref[...]
    y = jnp.dot(a, w_ref[...], preferred_element_type=F32) + b_ref[...]
    xn = x_ref[...] + gate_ref[0] * y
    if emit_x:
        xo_ref[...] = xn
    _norm_outputs(xn, norm_refs, out_refs)


def _out_proj(a, w, b, x, gate, norms, out_dtypes, seq, conv=None, emit_x=True, name="out_proj"):
    t, d = x.shape
    tm = min(TILES["res_rows"], seq)
    tpb = seq // tm
    row = pl.BlockSpec((tm, d), lambda i: (i, 0))
    full = lambda r, c: pl.BlockSpec((r, c), lambda i: (0, 0))
    in_specs, args, scratch = [], [], []
    width = 0
    if conv is not None:
        dw, dwb, lng, lnb = conv
        width = dw.shape[0]
        assert width - 1 <= HALO_ROWS and tm % HALO_ROWS == 0
        hpt = tm // HALO_ROWS
        in_specs += [row,
                     pl.BlockSpec((HALO_ROWS, d), lambda i: (jnp.maximum(i * hpt - 1, 0), 0)),
                     full(width, d), full(1, d), full(1, d), full(1, d)]
        args += [a, a, dw, dwb, lng, lnb]
        lc = min(TILES["conv_lanes"], d)
        scratch = [pltpu.VMEM((tm + HALO_ROWS, d), F32),
                   pltpu.VMEM((SUBLANES - 1, tm + HALO_ROWS, lc), F32),
                   pltpu.VMEM((width + 1, SUBLANES, lc), F32),
                   pltpu.VMEM((tm, d), F32)]
    else:
        in_specs += [row]
        args += [a]
    in_specs += [full(d, d), full(1, d), row, _vec_spec(d, tpb)]
    args += [w, b, x, gate]
    for g, sc, sh in norms:
        in_specs += [full(1, d), _vec_spec(d, tpb), _vec_spec(d, tpb)]
        args += [g, sc, sh]
    out_specs = [row] * (len(norms) + int(emit_x))
    out_shape = ([jax.ShapeDtypeStruct((t, d), F32)] if emit_x else []) + \
        [jax.ShapeDtypeStruct((t, d), dt) for dt in out_dtypes]
    return pl.pallas_call(
        functools.partial(_out_proj_kernel, conv_width=width, n_norms=len(norms),
                          emit_x=emit_x, rows_per_seq_tile=tpb),
        grid=(t // tm,),
        in_specs=in_specs, out_specs=out_specs, out_shape=out_shape,
        scratch_shapes=scratch,
        compiler_params=_cparams(1),
        name=name,
    )(*args)


def _cum_kernel(fl_ref, bf_ref, o_ref, *, blk):
    s = fl_ref.shape[0]
    r = lax.broadcasted_iota(I32, (blk, blk), 0)
    c = lax.broadcasted_iota(I32, (blk, blk), 1)
    tri = (r >= c).astype(BF16)
    carry = jnp.zeros((1, fl_ref.shape[1]), F32)
    for b0 in range(0, s, blk):
        z = fl_ref[b0:b0 + blk, :] + bf_ref[...]
        lf = jnp.minimum(z, 0.0) - jnp.log(1.0 + jnp.exp(-jnp.abs(z)))
        p0 = lf.astype(BF16)
        r1 = lf - p0.astype(F32)
        p1 = r1.astype(BF16)
        p2 = (r1 - p1.astype(F32)).astype(BF16)
        cs = (jnp.dot(tri, p0, preferred_element_type=F32)
              + jnp.dot(tri, p1, preferred_element_type=F32)
              + jnp.dot(tri, p2, preferred_element_type=F32)) + carry
        o_ref[b0:b0 + blk, :] = cs * LOG2E
        carry = cs[blk - 1:blk, :]


def _forget_cumsum(fl, bf, seq):
    t, n = fl.shape
    blk = min(TILES["cum_rows"], seq)
    return pl.pallas_call(
        functools.partial(_cum_kernel, blk=blk),
        grid=(t // seq,),
        in_specs=[pl.BlockSpec((seq, n), lambda b: (b, 0)),
                  pl.BlockSpec((1, n), lambda b: (0, 0))],
        out_specs=pl.BlockSpec((seq, n), lambda b: (b, 0)),
        out_shape=jax.ShapeDtypeStruct((t, n), F32),
        compiler_params=_cparams(1),
        name="forget_cumsum",
    )(fl, bf)


def _attn_kernel(q_ref, k_ref, v_ref, fs_ref, ft_ref, o_ref,
                 fsb_ref, vt_ref, m_ref, l_ref, acc_ref, *, tq, tk):
    i = pl.program_id(2)
    assert q_ref.shape[1] == LANES
    blocks_per_tile = tq // tk
    nt = (((1,), (1,)), ((), ()))

    def lanes(v, n):
        return v if n == LANES else jnp.concatenate([v] * (n // LANES), axis=1)

    @pl.when(i == 0)
    def _():
        fsb_ref[...] = jnp.broadcast_to(fs_ref[0], fsb_ref.shape)
        for j in range(vt_ref.shape[0]):
            vt_ref[j] = v_ref[j * tk:(j + 1) * tk, :].astype(F32).T.astype(BF16)

    m_ref[...] = jnp.full(m_ref.shape, -jnp.inf, F32)
    l_ref[...] = jnp.zeros(l_ref.shape, F32)
    acc_ref[...] = jnp.zeros(acc_ref.shape, F32)

    def update(j, q0, causal):
        nq = tq - q0
        k0 = pl.multiple_of(j * tk, tk)
        cols = slice(q0, tq)
        s = lax.dot_general(k_ref[pl.ds(k0, tk), :], q_ref[cols, :], nt, preferred_element_type=F32)
        s = s + ft_ref[0, :, cols] - lanes(fsb_ref[pl.ds(k0, tk), :], nq)
        if causal:
            r = lax.broadcasted_iota(I32, s.shape, 0)
            c = lax.broadcasted_iota(I32, s.shape, 1)
            s = jnp.where(c >= r, s, -jnp.inf)
        m = m_ref[:, cols]
        m_new = jnp.maximum(m, jnp.max(s, axis=0, keepdims=True))
        p = jnp.exp2(s - m_new)
        alpha = jnp.exp2(m - m_new)
        m_ref[:, cols] = m_new
        l_ref[:, cols] = alpha * l_ref[:, cols] + jnp.sum(p, axis=0, keepdims=True)
        acc_ref[:, cols] = alpha * acc_ref[:, cols] + jnp.dot(
            vt_ref[j], p.astype(BF16), preferred_element_type=F32)

    def key_tile(jt, _):
        for kb in range(blocks_per_tile):
            update(jt * blocks_per_tile + kb, 0, False)
        return 0

    lax.fori_loop(0, i, key_tile, 0)
    for kb in range(blocks_per_tile):
        update(i * blocks_per_tile + kb, kb * tk, True)
    o_ref[...] = (acc_ref[...] / l_ref[...]).T.astype(o_ref.dtype)


def _attention(q, kv, cum_rows, cum_cols, nb, seq, nh):
    t, d = q.shape
    dh = d // nh
    tq = min(TILES["attn_q"], seq)
    tk = min(TILES["attn_keys"], tq)
    nq = seq // tq
    return pl.pallas_call(
        functools.partial(_attn_kernel, tq=tq, tk=tk),
        grid=(nb, nh, nq),
        in_specs=[pl.BlockSpec((tq, dh), lambda b, h, i: (b * nq + i, h)),
                  pl.BlockSpec((seq, dh), lambda b, h, i: (b, h)),
                  pl.BlockSpec((seq, dh), lambda b, h, i: (b, nh + h)),
                  pl.BlockSpec((1, seq, 1), lambda b, h, i: (b * nh + h, 0, 0)),
                  pl.BlockSpec((1, 1, tq), lambda b, h, i: (b * nh + h, 0, i))],
        out_specs=pl.BlockSpec((tq, dh), lambda b, h, i: (b * nq + i, h)),
        out_shape=jax.ShapeDtypeStruct((t, d), BF16),
        scratch_shapes=[pltpu.VMEM((seq, LANES), F32), pltpu.VMEM((seq // tk, dh, tk), BF16),
                        pltpu.VMEM((1, tq), F32), pltpu.VMEM((1, tq), F32), pltpu.VMEM((dh, tq), F32)],
        compiler_params=_cparams(3),
        name="forgetting_attention",
    )(q, kv, kv, cum_cols, cum_rows.reshape(nb * nh, 1, seq))


def _ffn_kernel(te_ref, nv_ref, *refs, grouped, n_norms, emit_x, cast_w, chunk):
    it = iter(refs)
    h_ref, wg_ref, wu_ref, wd_ref = (next(it) for _ in range(4))
    if not grouped:
        x_ref, gate_ref = next(it), next(it)
        norm_refs = [next(it) for _ in range(3 * n_norms)]
        xo_ref = next(it) if emit_x else None
        out_refs = [next(it) for _ in range(n_norms)]
    else:
        y_ref = next(it)
    acc_ref = next(it)
    hb_ref = next(it) if grouped else h_ref
    i, f = pl.program_id(0), pl.program_id(1)
    nf = pl.num_programs(1)
    tm = acc_ref.shape[0]

    @pl.when(f == 0)
    def _():
        acc_ref[...] = jnp.zeros_like(acc_ref)
        if grouped:
            hi, lo = _unpack_halves(h_ref[...])
            hb_ref[...] = jnp.concatenate([hi, lo], axis=1).astype(BF16)

    def step():
        h = hb_ref[...]
        wg, wu, wd = wg_ref[0], wu_ref[0], wd_ref[0]
        if cast_w:
            wg, wu, wd = wg.astype(BF16), wu.astype(BF16), wd.astype(BF16)
        g = jnp.dot(h, wg, preferred_element_type=F32)
        u = jnp.dot(h, wu, preferred_element_type=F32)
        a = (_silu(g) * u).astype(BF16)
        acc_ref[...] += jnp.dot(a, wd, preferred_element_type=F32)

    if grouped:
        pl.when(nv_ref[i] > 0)(step)
    else:
        step()

    @pl.when(f == nf - 1)
    def _():
        if grouped:
            y_ref[...] = _pack_halves(acc_ref[...])
        else:
            def rows_chunk(c, _):
                r0 = pl.multiple_of(c * chunk, chunk)
                rs = pl.ds(r0, chunk)
                xn = x_ref[rs, :] + gate_ref[0] * acc_ref[rs, :]
                if emit_x:
                    xo_ref[rs, :] = xn
                _norm_outputs(xn, norm_refs, [o.at[rs, :] for o in out_refs])
                return 0
            lax.fori_loop(0, tm // chunk, rows_chunk, 0)


def _ffn_weight_specs(d, ff, tf, nf):
    def col(i, f, te, nv):
        return (te[i], 0, jnp.where(nv[i] > 0, f, nf - 1))

    def row(i, f, te, nv):
        return (te[i], jnp.where(nv[i] > 0, f, nf - 1), 0)
    return [pl.BlockSpec((1, d, tf), col), pl.BlockSpec((1, d, tf), col), pl.BlockSpec((1, tf, d), row)]


def _dense_ffn(h, wg, wu, wd, which, x, gate, norms, out_dtypes, seq, emit_x=True):
    t, d = x.shape
    ff = wg.shape[-1]
    tm = min(TILES["ffn_rows"], seq)
    tf = min(TILES["ffn_cols"], ff)
    nf = ff // tf
    tpb = seq // tm
    nt = t // tm
    row = pl.BlockSpec((tm, d), lambda i, f, te, nv: (i, 0))
    full = pl.BlockSpec((1, d), lambda i, f, te, nv: (0, 0))
    vec = pl.BlockSpec((1, 1, d), lambda i, f, te, nv: (i // tpb, 0, 0))
    in_specs = [row] + _ffn_weight_specs(d, ff, tf, nf) + [row, vec]
    args = [h, wg, wu, wd, x, gate]
    for g, sc, sh in norms:
        in_specs += [full, vec, vec]
        args += [g, sc, sh]
    out_specs = [row] * (len(norms) + int(emit_x))
    out_shape = ([jax.ShapeDtypeStruct((t, d), F32)] if emit_x else []) + \
        [jax.ShapeDtypeStruct((t, d), dt) for dt in out_dtypes]
    chunk = min(TILES["epi_chunk"], tm)
    return pl.pallas_call(
        functools.partial(_ffn_kernel, grouped=False, n_norms=len(norms), emit_x=emit_x,
                          cast_w=wg.dtype != BF16, chunk=chunk),
        grid_spec=pltpu.PrefetchScalarGridSpec(
            num_scalar_prefetch=2, grid=(nt, nf),
            in_specs=in_specs, out_specs=out_specs,
            scratch_shapes=[pltpu.VMEM((tm, d), F32)]),
        out_shape=out_shape,
        compiler_params=_cparams(2),
        name="dense_swiglu",
    )(jnp.full((nt,), which, I32), jnp.ones((nt,), I32), *args)


def _grouped_ffn(hs, wg, wu, wd, tile_expert, tile_rows, tm):
    r, dp = hs.shape
    d = 2 * dp
    ff = wg.shape[-1]
    tf = min(TILES["moe_cols"], ff)
    nf = ff // tf
    row = pl.BlockSpec((tm, dp), lambda i, f, te, nv: (i, 0))
    return pl.pallas_call(
        functools.partial(_ffn_kernel, grouped=True, n_norms=0, emit_x=False,
                          cast_w=wg.dtype != BF16, chunk=0),
        grid_spec=pltpu.PrefetchScalarGridSpec(
            num_scalar_prefetch=2, grid=(r // tm, nf),
            in_specs=[row] + _ffn_weight_specs(d, ff, tf, nf),
            out_specs=row,
            scratch_shapes=[pltpu.VMEM((tm, d), F32), pltpu.VMEM((tm, d), BF16)]),
        out_shape=jax.ShapeDtypeStruct((r, dp), U32),
        compiler_params=_cparams(2),
        name="expert_swiglu",
    )(tile_expert, tile_rows, hs, wg, wu, wd)


def _route_kernel(lg_ref, info_ref, cnt_ref, carry_ref, *, n_experts):
    tr, n = lg_ref.shape

    @pl.when(pl.program_id(0) == 0)
    def _():
        carry_ref[...] = jnp.zeros_like(carry_ref)

    lane = lax.broadcasted_iota(I32, (tr, n), 1).astype(F32)
    lg = jnp.where(lane < n_experts, lg_ref[...], -jnp.inf)
    v1 = jnp.max(lg, axis=-1, keepdims=True)
    i1 = jnp.min(jnp.where(lg == v1, lane, float(n)), axis=-1, keepdims=True)
    lg2 = jnp.where(lane == i1, -jnp.inf, lg)
    v2 = jnp.max(lg2, axis=-1, keepdims=True)
    i2 = jnp.min(jnp.where(lg2 == v2, lane, float(n)), axis=-1, keepdims=True)
    e = jnp.exp(v2 - v1)
    w1 = 1.0 / (1.0 + e)
    w2 = e / (1.0 + e)
    hot1 = lane == i1
    hot2 = lane == i2
    hot = jnp.where(hot1 | hot2, 1.0, 0.0)
    r = lax.broadcasted_iota(I32, (tr, tr), 0)
    c = lax.broadcasted_iota(I32, (tr, tr), 1)
    before = (r > c).astype(BF16)
    seen = jnp.dot(before, hot.astype(BF16), preferred_element_type=F32) + carry_ref[...]
    r1 = jnp.sum(jnp.where(hot1, seen, 0.0), axis=-1, keepdims=True)
    r2 = jnp.sum(jnp.where(hot2, seen, 0.0), axis=-1, keepdims=True)
    carry_ref[...] += jnp.sum(hot, axis=0, keepdims=True)
    cnt_ref[...] = carry_ref[...]
    cols = (i1, i2, w1, w2, r1, r2)
    out = jnp.zeros((tr, n), F32)
    for k, col in enumerate(cols):
        out = jnp.where(lane == float(k), col, out)
    info_ref[...] = out


def _route(logits, n_experts):
    t, n = logits.shape
    tr = min(TILES["route_rows"], t)
    return pl.pallas_call(
        functools.partial(_route_kernel, n_experts=n_experts),
        grid=(t // tr,),
        in_specs=[pl.BlockSpec((tr, n), lambda i: (i, 0))],
        out_specs=[pl.BlockSpec((tr, n), lambda i: (i, 0)), pl.BlockSpec((1, n), lambda i: (0, 0))],
        out_shape=[jax.ShapeDtypeStruct((t, n), F32), jax.ShapeDtypeStruct((1, n), F32)],
        scratch_shapes=[pltpu.VMEM((1, n), F32)],
        compiler_params=_cparams(1),
        name="route_top2",
    )(logits)


def _dispatch_kernel(pos_ref, tile_rows_ref, h_ref, hs_ref, buf_ref, zero_ref, sem, zsem):
    tr = h_ref.shape[0]
    base = pl.program_id(0) * (TOP_K * tr)

    @pl.when(pl.program_id(0) == 0)
    def _():
        zero_ref[...] = jnp.zeros_like(zero_ref)
        tm = zero_ref.shape[0]

        def fill(i):
            return pltpu.make_async_copy(zero_ref, hs_ref.at[pl.ds(i * tm, tm)], zsem)

        for i in range(tile_rows_ref.shape[0]):
            pl.when(tile_rows_ref[i] < tm)(lambda i=i: fill(i).start())
        for i in range(tile_rows_ref.shape[0]):
            pl.when(tile_rows_ref[i] < tm)(lambda i=i: fill(i).wait())

    buf_ref[...] = _pack_halves(h_ref[...].astype(F32))

    def row_copy(t, k):
        return pltpu.make_async_copy(buf_ref.at[pl.ds(t, 1)],
                                     hs_ref.at[pl.ds(pos_ref[base + TOP_K * t + k], 1)], sem)

    def start(t, _):
        for k in range(TOP_K):
            row_copy(t, k).start(priority=k)
        return 0

    def wait(t, _):
        for k in range(TOP_K):
            row_copy(t, k).wait()
        return 0

    lax.fori_loop(0, tr, start, 0, unroll=TILES["dma_unroll"])
    lax.fori_loop(0, tr, wait, 0, unroll=TILES["dma_unroll"])


def _dispatch(h, pos, tile_rows, tm):
    n_rows = tile_rows.shape[0] * tm
    t, d = h.shape
    tr = min(TILES["dispatch_rows"], t)
    return pl.pallas_call(
        _dispatch_kernel,
        grid_spec=pltpu.PrefetchScalarGridSpec(
            num_scalar_prefetch=2, grid=(t // tr,),
            in_specs=[pl.BlockSpec((tr, d), lambda i, pos, lt: (i, 0))],
            out_specs=pl.BlockSpec(memory_space=pl.ANY),
            scratch_shapes=[pltpu.VMEM((tr, d // 2), U32), pltpu.VMEM((tm, d // 2), U32),
                            pltpu.SemaphoreType.DMA, pltpu.SemaphoreType.DMA]),
        out_shape=jax.ShapeDtypeStruct((n_rows, d // 2), U32),
        compiler_params=_cparams(1),
        name="moe_dispatch",
    )(pos, tile_rows, h)


def _combine_kernel(pos_ref, *refs, n_norms, emit_x):
    it = iter(refs)
    info_ref, x_ref, gate_ref = (next(it) for _ in range(3))
    norm_refs = [next(it) for _ in range(3 * n_norms)]
    ys_ref = next(it)
    xo_ref = next(it) if emit_x else None
    out_refs = [next(it) for _ in range(n_norms)]
    buf_ref, sem = next(it), next(it)
    tc = x_ref.shape[0]
    base = pl.program_id(0) * (TOP_K * tc)

    def row_copy(t, k):
        return pltpu.make_async_copy(ys_ref.at[pl.ds(pos_ref[base + TOP_K * t + k], 1)],
                                     buf_ref.at[k, pl.ds(t, 1)], sem)

    def start(t, _):
        for k in range(TOP_K):
            row_copy(t, k).start(priority=k)
        return 0

    def wait(t, _):
        for k in range(TOP_K):
            row_copy(t, k).wait()
        return 0

    lax.fori_loop(0, tc, start, 0, unroll=TILES["dma_unroll"])
    lax.fori_loop(0, tc, wait, 0, unroll=TILES["dma_unroll"])
    info = info_ref[...]
    w1, w2 = info[:, 2:3], info[:, 3:4]
    a_hi, a_lo = _unpack_halves(buf_ref[0])
    b_hi, b_lo = _unpack_halves(buf_ref[1])
    y = jnp.concatenate([w1 * a_hi + w2 * b_hi, w1 * a_lo + w2 * b_lo], axis=1)
    xn = x_ref[...] + gate_ref[0] * y
    if emit_x:
        xo_ref[...] = xn
    _norm_outputs(xn, norm_refs, out_refs)


def _combine(ys, pos, info, x, gate, norms, out_dtypes, seq, emit_x=True):
    t, d = x.shape
    tc = min(TILES["combine_rows"], seq)
    tpb = seq // tc
    row = pl.BlockSpec((tc, d), lambda i, pos: (i, 0))
    full = pl.BlockSpec((1, d), lambda i, pos: (0, 0))
    vec = pl.BlockSpec((1, 1, d), lambda i, pos: (i // tpb, 0, 0))
    in_specs = [pl.BlockSpec((tc, info.shape[1]), lambda i, pos: (i, 0)), row, vec]
    args = [info, x, gate]
    for g, sc, sh in norms:
        in_specs += [full, vec, vec]
        args += [g, sc, sh]
    in_specs += [pl.BlockSpec(memory_space=pl.ANY)]
    args += [ys]
    out_specs = [row] * (len(norms) + int(emit_x))
    out_shape = ([jax.ShapeDtypeStruct((t, d), F32)] if emit_x else []) + \
        [jax.ShapeDtypeStruct((t, d), dt) for dt in out_dtypes]
    return pl.pallas_call(
        functools.partial(_combine_kernel, n_norms=len(norms), emit_x=emit_x),
        grid_spec=pltpu.PrefetchScalarGridSpec(
            num_scalar_prefetch=1, grid=(t // tc,),
            in_specs=in_specs, out_specs=out_specs,
            scratch_shapes=[pltpu.VMEM((TOP_K, tc, d // 2), U32), pltpu.SemaphoreType.DMA]),
        out_shape=out_shape,
        compiler_params=_cparams(1),
        name="moe_combine",
    )(pos, *args)


def _moe_layer(h, x, gate, norms, out_dtypes, seq, router_w, router_b, wg, wu, wd, first_expert, emit_x=True):
    t, d = h.shape
    ne = router_w.shape[1]
    tm = min(TILES["moe_rows"], t)
    rw = jnp.zeros((d, LANES), BF16).at[:, :ne].set(router_w.astype(BF16))
    rb = jnp.zeros((1, LANES), F32).at[0, :ne].set(router_b)
    logits = _proj(h, rw, rb, F32, name="router_logits")
    info, counts = _route(logits, ne)
    counts = counts[0, :ne].astype(I32)
    padded = (counts + tm - 1) // tm * tm
    ends = jnp.cumsum(padded)
    starts = ends - padded
    n_tiles = (TOP_K * t) // tm + ne
    tile_row0 = jnp.arange(n_tiles, dtype=I32) * tm
    tile_e = jnp.minimum(jnp.sum(tile_row0[:, None] >= ends[None, :], axis=1), ne - 1).astype(I32)
    tile_rows = jnp.clip(starts[tile_e] + counts[tile_e] - tile_row0, 0, tm)
    tile_rows = jnp.where(tile_row0 < ends[-1], tile_rows, 0).astype(I32)
    last_e = jnp.max(jnp.where(counts > 0, jnp.arange(ne, dtype=I32), 0))
    tile_e = jnp.where(tile_row0 < ends[-1], tile_e, last_e).astype(I32)
    e_idx = info[:, 0:TOP_K].astype(I32)
    rank = info[:, 4:4 + TOP_K].astype(I32)
    start_of = jnp.sum(jnp.where(e_idx[:, :, None] == jnp.arange(ne, dtype=I32), starts, 0), axis=-1)
    pos = (start_of + rank).reshape(-1).astype(I32)
    hs = _dispatch(h, pos, tile_rows, tm)
    ys = _grouped_ffn(hs, wg, wu, wd, tile_e + first_expert, tile_rows, tm)
    return _combine(ys, pos, info, x, gate, norms, out_dtypes, seq, emit_x=emit_x)


def kernel(x, c, ada_w, ada_b, norm_mix_g, norm_ffn_g, conv_pw1_w, conv_pw1_b, conv_dw_w, conv_dw_b, conv_ln_g, conv_ln_b, conv_pw2_w, conv_pw2_b, kv_ada_w, kv_ada_b, kv_norm_g, w_kvf, b_f, attn_wq, attn_wo, ffn_w_gate, ffn_w_up, ffn_w_down, moe_router_w, moe_router_b, moe_w_gate, moe_w_up, moe_w_down, final_norm_g):
    nb, seq, d = x.shape
    depth = ada_w.shape[0]
    n_conv = conv_pw1_w.shape[0]
    nh = b_f.shape[0]
    t = nb * seq
    xs = x.reshape(t, d)

    mods = _ada(c, ada_w, ada_b)
    kv_mod = _ada(c, kv_ada_w[None], kv_ada_b[None])[0]

    def vec(v):
        return v.reshape(nb, 1, d)

    def mod(layer, k):
        return vec(mods[layer, :, k * d:(k + 1) * d])

    row = lambda v: v.reshape(1, d)
    zero_vec = jnp.zeros((nb, 1, d), F32)

    def mix_norm(layer):
        return (row(norm_mix_g[layer]), mod(layer, 1), mod(layer, 0))

    def ffn_norm(layer):
        return (row(norm_ffn_g[layer]), mod(layer, 4), mod(layer, 3))

    def after_ffn(layer):
        if layer == depth - 1:
            return [(row(final_norm_g), zero_vec, zero_vec)], [F32]
        norms, dts = [mix_norm(layer + 1)], [BF16]
        if layer + 1 == n_conv:
            norms.append((row(kv_norm_g), vec(kv_mod[:, d:]), vec(kv_mod[:, :d])))
            dts.append(BF16)
        return norms, dts

    ffn_g, ffn_u, ffn_d = (w.astype(BF16) for w in (ffn_w_gate, ffn_w_up, ffn_w_down))
    h = _first_norm(xs, *mix_norm(0), seq)
    kv = cum_rows = cum_cols = None
    out = None
    for layer in range(depth):
        g1, g2 = mod(layer, 2), mod(layer, 5)
        if layer < n_conv:
            i = layer
            u = _glu_proj(h, conv_pw1_w[i].astype(BF16), conv_pw1_b[i].reshape(1, 2 * d))
            xs, h = _out_proj(u, conv_pw2_w[i].astype(BF16), row(conv_pw2_b[i]), xs, g1,
                              [ffn_norm(layer)], [BF16], seq,
                              conv=(conv_dw_w[i], row(conv_dw_b[i]), row(conv_ln_g[i]), row(conv_ln_b[i])),
                              name="conv_out_proj")
        else:
            i = layer - n_conv
            q = _proj(h, attn_wq[i].astype(BF16), jnp.zeros((1, d), F32), BF16,
                      scale=float(d // nh) ** -0.5 * LOG2E, name="attn_q_proj")
            o = _attention(q, kv, cum_rows, cum_cols, nb, seq, nh)
            xs, h = _out_proj(o, attn_wo[i].astype(BF16), jnp.zeros((1, d), F32), xs, g1,
                              [ffn_norm(layer)], [BF16], seq, name="attn_out_proj")
        norms, dts = after_ffn(layer)
        emit_x = layer != depth - 1
        j = layer // 2
        if layer % 2 == 0:
            outs = _dense_ffn(h, ffn_g, ffn_u, ffn_d, j, xs, g2, norms, dts, seq, emit_x=emit_x)
        else:
            ne, ff = moe_w_gate.shape[1], moe_w_gate.shape[3]
            outs = _moe_layer(h, xs, g2, norms, dts, seq, moe_router_w[j], moe_router_b[j],
                              moe_w_gate.reshape(-1, d, ff), moe_w_up.reshape(-1, d, ff),
                              moe_w_down.reshape(-1, ff, d), j * ne, emit_x=emit_x)
        if emit_x:
            xs, h = outs[0], outs[1]
        else:
            out = outs[0]
        if layer + 1 == n_conv:
            h_kv = outs[2]
            kv = _proj(h_kv, w_kvf[:, :2 * d].astype(BF16), jnp.zeros((1, 2 * d), F32), BF16,
                       name="shared_kv_proj")
            wf = jnp.zeros((d, LANES), BF16).at[:, :nh].set(w_kvf[:, 2 * d:].astype(BF16))
            fl = _proj(h_kv, wf, jnp.zeros((1, LANES), F32), F32, name="forget_logits")
            bf = jnp.zeros((1, LANES), F32).at[0, :nh].set(b_f)
            cum = _forget_cumsum(fl, bf, seq)[:, :nh].reshape(nb, seq, nh)
            cum_rows = cum.transpose(0, 2, 1).reshape(nb * nh, seq)
            cum_cols = cum_rows.reshape(nb * nh, seq, 1)
    return out.reshape(nb, seq, d)
```

```python
import functools

import jax
import jax.numpy as jnp
from jax import lax
from jax.experimental import pallas as pl
from jax.experimental.pallas import tpu as pltpu

F32 = jnp.float32
BF16 = jnp.bfloat16
U32 = jnp.uint32
I32 = jnp.int32

EPS = 1e-6
TOP_K = 2
LANES = 128
MXU_DIM = 256
HALO_ROWS = 32
HI_MASK = 0xFFFF0000
SUBLANES = 8
LOG2E = 1.4426950408889634

TILES = dict(
    norm_rows=512,
    mm_rows=1024, mm_cols=512,
    res_rows=256,
    attn_q=512, attn_keys=512, attn_heads=4,
    conv_rows=32, conv_lanes=512,
    dma_unroll=8,
    ffn_rows=512, ffn_cols=512,
    moe_rows=1024, moe_cols=256,
    route_rows=512,
    dispatch_rows=512,
    combine_rows=256,
    ada_cols=1024,
    cum_rows=256,
    epi_chunk=128,
)
VMEM_LIMIT = 56 * 2**20


def _cparams(n_axes, vmem=VMEM_LIMIT):
    return pltpu.CompilerParams(dimension_semantics=("arbitrary",) * n_axes,
                                vmem_limit_bytes=vmem)


def _sigmoid(v):
    return 1.0 / (1.0 + jnp.exp(-v))


def _silu(v):
    return v * _sigmoid(v)


def _pack_halves(v):
    n = v.shape[1] // 2
    hi = lax.bitcast_convert_type(v[:, :n].astype(BF16).astype(F32), U32)
    lo = lax.bitcast_convert_type(v[:, n:].astype(BF16).astype(F32), U32)
    return hi | lax.shift_right_logical(lo, jnp.uint32(16))


def _unpack_halves(p):
    hi = lax.bitcast_convert_type(p & jnp.uint32(HI_MASK), F32)
    lo = lax.bitcast_convert_type(lax.shift_left(p, jnp.uint32(16)), F32)
    return hi, lo


def _norm_outputs(xn, norm_refs, out_refs):
    ms = jnp.mean(xn * xn, axis=-1, keepdims=True)
    xr = xn * lax.rsqrt(ms + EPS)
    for k, o_ref in enumerate(out_refs):
        g = norm_refs[3 * k][...]
        sc = norm_refs[3 * k + 1][0]
        sh = norm_refs[3 * k + 2][0]
        o_ref[...] = (xr * g * (1.0 + sc) + sh).astype(o_ref.dtype)


def _ada_kernel(c_ref, w_ref, b_ref, o_ref):
    ca = _silu(c_ref[...]).astype(BF16)
    o_ref[0] = jnp.dot(ca, w_ref[0].astype(BF16), preferred_element_type=F32) + b_ref[0]


def _ada(c, w, b):
    nl, d, n = w.shape
    nb = c.shape[0]
    tn = min(TILES["ada_cols"], n)
    return pl.pallas_call(
        _ada_kernel,
        grid=(nl, n // tn),
        in_specs=[pl.BlockSpec((nb, d), lambda l, j: (0, 0)),
                  pl.BlockSpec((1, d, tn), lambda l, j: (l, 0, j)),
                  pl.BlockSpec((1, 1, tn), lambda l, j: (l, 0, j))],
        out_specs=pl.BlockSpec((1, nb, tn), lambda l, j: (l, 0, j)),
        out_shape=jax.ShapeDtypeStruct((nl, nb, n), F32),
        compiler_params=_cparams(2),
        name="ada_modulation",
    )(c, w, b.reshape(nl, 1, n))


def _norm_kernel(x_ref, g_ref, sc_ref, sh_ref, o_ref):
    _norm_outputs(x_ref[...], (g_ref, sc_ref, sh_ref), (o_ref,))


def _vec_spec(d, rows_per_batch_tile):
    return pl.BlockSpec((1, 1, d), lambda i, *_: (i // rows_per_batch_tile, 0, 0))


def _first_norm(x, g, sc, sh, seq):
    t, d = x.shape
    tm = min(TILES["norm_rows"], seq)
    tpb = seq // tm
    return pl.pallas_call(
        _norm_kernel,
        grid=(t // tm,),
        in_specs=[pl.BlockSpec((tm, d), lambda i: (i, 0)),
                  pl.BlockSpec((1, d), lambda i: (0, 0)),
                  _vec_spec(d, tpb), _vec_spec(d, tpb)],
        out_specs=pl.BlockSpec((tm, d), lambda i: (i, 0)),
        out_shape=jax.ShapeDtypeStruct((t, d), BF16),
        compiler_params=_cparams(1),
        name="first_norm",
    )(x, g, sc, sh)


def _proj_kernel(a_ref, w_ref, b_ref, o_ref, *, scale):
    y = jnp.dot(a_ref[...], w_ref[0], preferred_element_type=F32) + b_ref[...]
    if scale != 1.0:
        y = y * scale
    o_ref[...] = y.astype(o_ref.dtype)


def _proj(a, w, b, out_dtype, scale=1.0, name="proj"):
    t, d = a.shape
    n = w.shape[1]
    tm = min(TILES["mm_rows"], t)
    tn = min(TILES["mm_cols"], n)
    return pl.pallas_call(
        functools.partial(_proj_kernel, scale=scale),
        grid=(t // tm, n // tn),
        in_specs=[pl.BlockSpec((tm, d), lambda i, j: (i, 0)),
                  pl.BlockSpec((1, d, tn), lambda i, j: (j, 0, 0)),
                  pl.BlockSpec((1, tn), lambda i, j: (0, j))],
        out_specs=pl.BlockSpec((tm, tn), lambda i, j: (i, j)),
        out_shape=jax.ShapeDtypeStruct((t, n), out_dtype),
        compiler_params=_cparams(2),
        name=name,
    )(a, _column_blocks(w[None], tn), b)


def _glu_kernel(a_ref, wa_ref, wb_ref, ba_ref, bb_ref, o_ref):
    a = a_ref[...]
    lin = jnp.dot(a, wa_ref[0], preferred_element_type=F32) + ba_ref[...]
    gate = jnp.dot(a, wb_ref[0], preferred_element_type=F32) + bb_ref[...]
    o_ref[...] = (lin * _sigmoid(gate)).astype(o_ref.dtype)


def _glu_proj(a, w, b):
    t, d = a.shape
    tm = min(TILES["mm_rows"], t)
    tn = min(TILES["mm_cols"], d)
    nj = d // tn
    wb = _column_blocks(w[None], tn)
    return pl.pallas_call(
        _glu_kernel,
        grid=(t // tm, nj),
        in_specs=[pl.BlockSpec((tm, d), lambda i, j: (i, 0)),
                  pl.BlockSpec((1, d, tn), lambda i, j: (j, 0, 0)),
                  pl.BlockSpec((1, d, tn), lambda i, j: (j + nj, 0, 0)),
                  pl.BlockSpec((1, tn), lambda i, j: (0, j)),
                  pl.BlockSpec((1, tn), lambda i, j: (0, j + nj))],
        out_specs=pl.BlockSpec((tm, tn), lambda i, j: (i, j)),
        out_shape=jax.ShapeDtypeStruct((t, d), BF16),
        compiler_params=_cparams(2),
        name="conv_pw1_glu",
    )(a, wb, wb, b, b)


def _causal_conv_ln_swish(u_ref, halo_ref, dw_ref, dwb_ref, lng_ref, lnb_ref,
                          win_ref, shift_ref, taps_ref, conv_ref, first_in_seq, width):
    tm, d = u_ref.shape
    n = tm + HALO_ROWS
    halo = halo_ref[...].astype(F32)
    win_ref[0:HALO_ROWS, :] = jnp.where(first_in_seq, 0.0, halo)
    win_ref[HALO_ROWS:, :] = u_ref[...].astype(F32)
    base = HALO_ROWS - (width - 1)
    rc = min(TILES["conv_rows"], tm)
    lc = shift_ref.shape[2]
    for c0 in range(0, d, lc):
        cols = slice(c0, c0 + lc)
        for b in range(1, SUBLANES):
            shift_ref[b - 1, 0:n - SUBLANES, :] = win_ref[b:n - SUBLANES + b, cols]

        for k in range(width):
            taps_ref[k] = jnp.broadcast_to(dw_ref[k:k + 1, cols], (SUBLANES, lc))
        taps_ref[width] = jnp.broadcast_to(dwb_ref[:, cols], (SUBLANES, lc))

        def rows_chunk(ci, _, cols=cols):
            r0 = ci * rc
            groups = range(rc // SUBLANES)
            acc = [taps_ref[width] for _ in groups]
            for k in range(width):
                blk, b = divmod(base + k, SUBLANES)
                w = taps_ref[k]
                for g in groups:
                    rs = pl.ds(pl.multiple_of(r0 + (blk + g) * SUBLANES, SUBLANES), SUBLANES)
                    src = win_ref[rs, cols] if b == 0 else shift_ref[b - 1, rs, :]
                    acc[g] = acc[g] + src * w
            for g in groups:
                conv_ref[pl.ds(pl.multiple_of(r0 + g * SUBLANES, SUBLANES), SUBLANES), cols] = acc[g]
            return 0

        lax.fori_loop(0, tm // rc, rows_chunk, 0)
    v = conv_ref[...]
    mu = jnp.mean(v, axis=-1, keepdims=True)
    vc = v - mu
    var = jnp.mean(vc * vc, axis=-1, keepdims=True)
    y = vc * lax.rsqrt(var + EPS) * lng_ref[...] + lnb_ref[...]
    return _silu(y).astype(BF16)


def _out_proj_kernel(*refs, conv_width, n_norms, emit_x, rows_per_seq_tile):
    it = iter(refs)
    if conv_width:
        u_ref, halo_ref, dw_ref, dwb_ref, lng_ref, lnb_ref = (next(it) for _ in range(6))
    else:
        a_ref = next(it)
    w_ref, b_ref, x_ref, gate_ref = (next(it) for _ in range(4))
    norm_refs = [next(it) for _ in range(3 * n_norms)]
    xo_ref = next(it) if emit_x else None
    out_refs = [next(it) for _ in range(n_norms)]
    if conv_width:
        win_ref, shift_ref, taps_ref, conv_ref = (next(it) for _ in range(4))
        first = (pl.program_id(0) % rows_per_seq_tile) == 0
        a = _causal_conv_ln_swish(u_ref, halo_ref, dw_ref, dwb_ref, lng_ref, lnb_ref,
                                  win_ref, shift_ref, taps_ref, conv_ref, first, conv_width)
    else:
        a = a_ref[...]
    y = jnp.dot(a, w_ref[...], preferred_element_type=F32) + b_ref[...]
    xn = x_ref[...] + gate_ref[0] * y
    if emit_x:
        xo_ref[...] = xn
    _norm_outputs(xn, norm_refs, out_refs)


def _out_proj(a, w, b, x, gate, norms, out_dtypes, seq, conv=None, emit_x=True, name="out_proj"):
    t, d = x.shape
    tm = min(TILES["res_rows"], seq)
    tpb = seq // tm
    row = pl.BlockSpec((tm, d), lambda i: (i, 0))
    full = lambda r, c: pl.BlockSpec((r, c), lambda i: (0, 0))
    in_specs, args, scratch = [], [], []
    width = 0
    if conv is not None:
        dw, dwb, lng, lnb = conv
        width = dw.shape[0]
        assert width - 1 <= HALO_ROWS and tm % HALO_ROWS == 0
        hpt = tm // HALO_ROWS
        in_specs += [row,
                     pl.BlockSpec((HALO_ROWS, d), lambda i: (jnp.maximum(i * hpt - 1, 0), 0)),
                     full(width, d), full(1, d), full(1, d), full(1, d)]
        args += [a, a, dw, dwb, lng, lnb]
        lc = min(TILES["conv_lanes"], d)
        scratch = [pltpu.VMEM((tm + HALO_ROWS, d), F32),
                   pltpu.VMEM((SUBLANES - 1, tm + HALO_ROWS, lc), F32),
                   pltpu.VMEM((width + 1, SUBLANES, lc), F32),
                   pltpu.VMEM((tm, d), F32)]
    else:
        in_specs += [row]
        args += [a]
    in_specs += [full(d, d), full(1, d), row, _vec_spec(d, tpb)]
    args += [w, b, x, gate]
    for g, sc, sh in norms:
        in_specs += [full(1, d), _vec_spec(d, tpb), _vec_spec(d, tpb)]
        args += [g, sc, sh]
    out_specs = [row] * (len(norms) + int(emit_x))
    out_shape = ([jax.ShapeDtypeStruct((t, d), F32)] if emit_x else []) + \
        [jax.ShapeDtypeStruct((t, d), dt) for dt in out_dtypes]
    return pl.pallas_call(
        functools.partial(_out_proj_kernel, conv_width=width, n_norms=len(norms),
                          emit_x=emit_x, rows_per_seq_tile=tpb),
        grid=(t // tm,),
        in_specs=in_specs, out_specs=out_specs, out_shape=out_shape,
        scratch_shapes=scratch,
        compiler_params=_cparams(1),
        name=name,
    )(*args)


def _cum_kernel(fl_ref, bf_ref, o_ref, *, blk):
    s = fl_ref.shape[0]
    r = lax.broadcasted_iota(I32, (blk, blk), 0)
    c = lax.broadcasted_iota(I32, (blk, blk), 1)
    tri = (r >= c).astype(BF16)
    carry = jnp.zeros((1, fl_ref.shape[1]), F32)
    for b0 in range(0, s, blk):
        z = fl_ref[b0:b0 + blk, :] + bf_ref[...]
        lf = jnp.minimum(z, 0.0) - jnp.log(1.0 + jnp.exp(-jnp.abs(z)))
        p0 = lf.astype(BF16)
        r1 = lf - p0.astype(F32)
        p1 = r1.astype(BF16)
        p2 = (r1 - p1.astype(F32)).astype(BF16)
        cs = (jnp.dot(tri, p0, preferred_element_type=F32)
              + jnp.dot(tri, p1, preferred_element_type=F32)
              + jnp.dot(tri, p2, preferred_element_type=F32)) + carry
        o_ref[b0:b0 + blk, :] = cs * LOG2E
        carry = cs[blk - 1:blk, :]


def _forget_cumsum(fl, bf, seq):
    t, n = fl.shape
    blk = min(TILES["cum_rows"], seq)
    return pl.pallas_call(
        functools.partial(_cum_kernel, blk=blk),
        grid=(t // seq,),
        in_specs=[pl.BlockSpec((seq, n), lambda b: (b, 0)),
                  pl.BlockSpec((1, n), lambda b: (0, 0))],
        out_specs=pl.BlockSpec((seq, n), lambda b: (b, 0)),
        out_shape=jax.ShapeDtypeStruct((t, n), F32),
        compiler_params=_cparams(1),
        name="forget_cumsum",
    )(fl, bf)


def _attn_kernel(q_ref, k_ref, v_ref, fs_ref, ft_ref, o_ref,
                 fsb_ref, vt_ref, m_ref, l_ref, acc_ref, *, tq, tk):
    i = pl.program_id(2)
    assert q_ref.shape[1] == fs_ref.shape[0] * LANES
    blocks_per_tile = tq // tk
    nt = (((1,), (1,)), ((), ()))

    def lanes(v, n):
        return v if n == LANES else jnp.concatenate([v] * (n // LANES), axis=1)

    heads = range(fs_ref.shape[0])
    head_lanes = [slice(h * LANES, (h + 1) * LANES) for h in heads]

    @pl.when(i == 0)
    def _():
        for h in heads:
            fsb_ref[h] = jnp.broadcast_to(fs_ref[h], fsb_ref.shape[1:])
            for j in range(vt_ref.shape[1]):
                vt_ref[h, j] = v_ref[j * tk:(j + 1) * tk, head_lanes[h]].astype(F32).T.astype(BF16)

    m_ref[...] = jnp.full(m_ref.shape, -jnp.inf, F32)
    l_ref[...] = jnp.zeros(l_ref.shape, F32)
    acc_ref[...] = jnp.zeros(acc_ref.shape, F32)

    def update(h, j, q0, causal):
        nq = tq - q0
        k0 = pl.multiple_of(j * tk, tk)
        cols = slice(q0, tq)
        s = lax.dot_general(k_ref[pl.ds(k0, tk), head_lanes[h]], q_ref[cols, head_lanes[h]], nt,
                            preferred_element_type=F32)
        s = s + ft_ref[h, :, cols] - lanes(fsb_ref[h, pl.ds(k0, tk), :], nq)
        if causal:
            r = lax.broadcasted_iota(I32, s.shape, 0)
            c = lax.broadcasted_iota(I32, s.shape, 1)
            s = jnp.where(c >= r, s, -jnp.inf)
        m = m_ref[h, :, cols]
        m_new = jnp.maximum(m, jnp.max(s, axis=0, keepdims=True))
        p = jnp.exp2(s - m_new)
        alpha = jnp.exp2(m - m_new)
        m_ref[h, :, cols] = m_new
        l_ref[h, :, cols] = alpha * l_ref[h, :, cols] + jnp.sum(p, axis=0, keepdims=True)
        acc_ref[h, :, cols] = alpha * acc_ref[h, :, cols] + jnp.dot(
            vt_ref[h, j], p.astype(BF16), preferred_element_type=F32)

    def key_tile(jt, _):
        for kb in range(blocks_per_tile):
            for h in heads:
                update(h, jt * blocks_per_tile + kb, 0, False)
        return 0

    lax.fori_loop(0, i, key_tile, 0)
    for kb in range(blocks_per_tile):
        for h in heads:
            update(h, i * blocks_per_tile + kb, kb * tk, True)
    for h in heads:
        o_ref[:, head_lanes[h]] = (acc_ref[h] / l_ref[h]).T.astype(o_ref.dtype)


def _attention(q, kv, cum_rows, cum_cols, nb, seq, nh):
    t, d = q.shape
    dh = d // nh
    tq = min(TILES["attn_q"], seq)
    tk = min(TILES["attn_keys"], tq)
    hg = min(TILES["attn_heads"], nh)
    nq = seq // tq
    ng = nh // hg
    return pl.pallas_call(
        functools.partial(_attn_kernel, tq=tq, tk=tk),
        grid=(nb, ng, nq),
        in_specs=[pl.BlockSpec((tq, hg * dh), lambda b, g, i: (b * nq + i, g)),
                  pl.BlockSpec((seq, hg * dh), lambda b, g, i: (b, g)),
                  pl.BlockSpec((seq, hg * dh), lambda b, g, i: (b, ng + g)),
                  pl.BlockSpec((hg, seq, 1), lambda b, g, i: (b * ng + g, 0, 0)),
                  pl.BlockSpec((hg, 1, tq), lambda b, g, i: (b * ng + g, 0, i))],
        out_specs=pl.BlockSpec((tq, hg * dh), lambda b, g, i: (b * nq + i, g)),
        out_shape=jax.ShapeDtypeStruct((t, d), BF16),
        scratch_shapes=[pltpu.VMEM((hg, seq, LANES), F32), pltpu.VMEM((hg, seq // tk, dh, tk), BF16),
                        pltpu.VMEM((hg, 1, tq), F32), pltpu.VMEM((hg, 1, tq), F32),
                        pltpu.VMEM((hg, dh, tq), F32)],
        compiler_params=_cparams(3),
        name="forgetting_attention",
    )(q, kv, kv, cum_cols, cum_rows.reshape(nb * nh, 1, seq))


def _ffn_kernel(te_ref, nv_ref, *refs, grouped, n_norms, emit_x, cast_w, chunk):
    it = iter(refs)
    h_ref, wg_ref, wu_ref, wd_ref = (next(it) for _ in range(4))
    if not grouped:
        x_ref, gate_ref = next(it), next(it)
        norm_refs = [next(it) for _ in range(3 * n_norms)]
        xo_ref = next(it) if emit_x else None
        out_refs = [next(it) for _ in range(n_norms)]
    else:
        y_ref = next(it)
    acc_ref = next(it)
    hb_ref = next(it) if grouped else h_ref
    i, f = pl.program_id(0), pl.program_id(1)
    nf = pl.num_programs(1)
    tm = acc_ref.shape[0]

    @pl.when(f == 0)
    def _():
        acc_ref[...] = jnp.zeros_like(acc_ref)
        if grouped:
            hi, lo = _unpack_halves(h_ref[...])
            hb_ref[...] = jnp.concatenate([hi, lo], axis=1).astype(BF16)

    def step():
        h = hb_ref[...]
        wg, wu, wd = wg_ref[0], wu_ref[0], wd_ref[0]
        if cast_w:
            wg, wu, wd = wg.astype(BF16), wu.astype(BF16), wd.astype(BF16)
        g = jnp.dot(h, wg, preferred_element_type=F32)
        u = jnp.dot(h, wu, preferred_element_type=F32)
        a = (_silu(g) * u).astype(BF16)
        acc_ref[...] += jnp.dot(a, wd, preferred_element_type=F32)

    if grouped:
        pl.when(nv_ref[i] > 0)(step)
    else:
        step()

    @pl.when(f == nf - 1)
    def _():
        if grouped:
            y_ref[...] = _pack_halves(acc_ref[...])
        else:
            def rows_chunk(c, _):
                r0 = pl.multiple_of(c * chunk, chunk)
                rs = pl.ds(r0, chunk)
                xn = x_ref[rs, :] + gate_ref[0] * acc_ref[rs, :]
                if emit_x:
                    xo_ref[rs, :] = xn
                _norm_outputs(xn, norm_refs, [o.at[rs, :] for o in out_refs])
                return 0
            lax.fori_loop(0, tm // chunk, rows_chunk, 0)


def _ffn_weight_specs(d, ff, tf, nf, col_blocked):
    def col(i, f, te, nv):
        f = jnp.where(nv[i] > 0, f, nf - 1)
        return (te[i] * nf + f, 0, 0) if col_blocked else (te[i], 0, f)

    def row(i, f, te, nv):
        return (te[i], jnp.where(nv[i] > 0, f, nf - 1), 0)
    return [pl.BlockSpec((1, d, tf), col), pl.BlockSpec((1, d, tf), col), pl.BlockSpec((1, tf, d), row)]


def _column_blocks(w, tf):
    e, d, ff = w.shape
    return w.reshape(e, d, ff // tf, tf).transpose(0, 2, 1, 3).reshape(e * (ff // tf), d, tf)


def _dense_ffn(h, wg, wu, wd, which, x, gate, norms, out_dtypes, seq, emit_x=True):
    t, d = x.shape
    ff, tf = wd.shape[1], wg.shape[-1]
    tm = min(TILES["ffn_rows"], seq)
    nf = ff // tf
    tpb = seq // tm
    nt = t // tm
    row = pl.BlockSpec((tm, d), lambda i, f, te, nv: (i, 0))
    full = pl.BlockSpec((1, d), lambda i, f, te, nv: (0, 0))
    vec = pl.BlockSpec((1, 1, d), lambda i, f, te, nv: (i // tpb, 0, 0))
    in_specs = [row] + _ffn_weight_specs(d, ff, tf, nf, True) + [row, vec]
    args = [h, wg, wu, wd, x, gate]
    for g, sc, sh in norms:
        in_specs += [full, vec, vec]
        args += [g, sc, sh]
    out_specs = [row] * (len(norms) + int(emit_x))
    out_shape = ([jax.ShapeDtypeStruct((t, d), F32)] if emit_x else []) + \
        [jax.ShapeDtypeStruct((t, d), dt) for dt in out_dtypes]
    chunk = min(TILES["epi_chunk"], tm)
    return pl.pallas_call(
        functools.partial(_ffn_kernel, grouped=False, n_norms=len(norms), emit_x=emit_x,
                          cast_w=wg.dtype != BF16, chunk=chunk),
        grid_spec=pltpu.PrefetchScalarGridSpec(
            num_scalar_prefetch=2, grid=(nt, nf),
            in_specs=in_specs, out_specs=out_specs,
            scratch_shapes=[pltpu.VMEM((tm, d), F32)]),
        out_shape=out_shape,
        compiler_params=_cparams(2),
        name="dense_swiglu",
    )(jnp.full((nt,), which, I32), jnp.ones((nt,), I32), *args)


def _grouped_ffn(hs, wg, wu, wd, tile_expert, tile_rows, tm):
    r, dp = hs.shape
    d = 2 * dp
    ff = wg.shape[-1]
    tf = min(TILES["moe_cols"], ff)
    nf = ff // tf
    row = pl.BlockSpec((tm, dp), lambda i, f, te, nv: (i, 0))
    return pl.pallas_call(
        functools.partial(_ffn_kernel, grouped=True, n_norms=0, emit_x=False,
                          cast_w=wg.dtype != BF16, chunk=0),
        grid_spec=pltpu.PrefetchScalarGridSpec(
            num_scalar_prefetch=2, grid=(r // tm, nf),
            in_specs=[row] + _ffn_weight_specs(d, ff, tf, nf, False),
            out_specs=row,
            scratch_shapes=[pltpu.VMEM((tm, d), F32), pltpu.VMEM((tm, d), BF16)]),
        out_shape=jax.ShapeDtypeStruct((r, dp), U32),
        compiler_params=_cparams(2),
        name="expert_swiglu",
    )(tile_expert, tile_rows, hs, wg, wu, wd)


def _route_kernel(lg_ref, info_ref, cnt_ref, carry_ref, *, n_experts):
    tr, n = lg_ref.shape

    @pl.when(pl.program_id(0) == 0)
    def _():
        carry_ref[...] = jnp.zeros_like(carry_ref)

    lane = lax.broadcasted_iota(I32, (tr, n), 1).astype(F32)
    lg = jnp.where(lane < n_experts, lg_ref[...], -jnp.inf)
    v1 = jnp.max(lg, axis=-1, keepdims=True)
    i1 = jnp.min(jnp.where(lg == v1, lane, float(n)), axis=-1, keepdims=True)
    lg2 = jnp.where(lane == i1, -jnp.inf, lg)
    v2 = jnp.max(lg2, axis=-1, keepdims=True)
    i2 = jnp.min(jnp.where(lg2 == v2, lane, float(n)), axis=-1, keepdims=True)
    e = jnp.exp(v2 - v1)
    w1 = 1.0 / (1.0 + e)
    w2 = e / (1.0 + e)
    hot1 = lane == i1
    hot2 = lane == i2
    hot = jnp.where(hot1 | hot2, 1.0, 0.0)
    r = lax.broadcasted_iota(I32, (tr, tr), 0)
    c = lax.broadcasted_iota(I32, (tr, tr), 1)
    before = (r > c).astype(BF16)
    seen = jnp.dot(before, hot.astype(BF16), preferred_element_type=F32) + carry_ref[...]
    r1 = jnp.sum(jnp.where(hot1, seen, 0.0), axis=-1, keepdims=True)
    r2 = jnp.sum(jnp.where(hot2, seen, 0.0), axis=-1, keepdims=True)
    carry_ref[...] += jnp.sum(hot, axis=0, keepdims=True)
    cnt_ref[...] = carry_ref[...]
    cols = (i1, i2, w1, w2, r1, r2)
    out = jnp.zeros((tr, n), F32)
    for k, col in enumerate(cols):
        out = jnp.where(lane == float(k), col, out)
    info_ref[...] = out


def _route(logits, n_experts):
    t, n = logits.shape
    tr = min(TILES["route_rows"], t)
    return pl.pallas_call(
        functools.partial(_route_kernel, n_experts=n_experts),
        grid=(t // tr,),
        in_specs=[pl.BlockSpec((tr, n), lambda i: (i, 0))],
        out_specs=[pl.BlockSpec((tr, n), lambda i: (i, 0)), pl.BlockSpec((1, n), lambda i: (0, 0))],
        out_shape=[jax.ShapeDtypeStruct((t, n), F32), jax.ShapeDtypeStruct((1, n), F32)],
        scratch_shapes=[pltpu.VMEM((1, n), F32)],
        compiler_params=_cparams(1),
        name="route_top2",
    )(logits)


def _dispatch_kernel(pos_ref, tile_rows_ref, h_ref, hs_ref, buf_ref, zero_ref, sem, zsem):
    tr = h_ref.shape[0]
    base = pl.program_id(0) * (TOP_K * tr)

    @pl.when(pl.program_id(0) == 0)
    def _():
        zero_ref[...] = jnp.zeros_like(zero_ref)
        tm = zero_ref.shape[0]

        def fill(i):
            return pltpu.make_async_copy(zero_ref, hs_ref.at[pl.ds(i * tm, tm)], zsem)

        for i in range(tile_rows_ref.shape[0]):
            pl.when(tile_rows_ref[i] < tm)(lambda i=i: fill(i).start())
        for i in range(tile_rows_ref.shape[0]):
            pl.when(tile_rows_ref[i] < tm)(lambda i=i: fill(i).wait())

    buf_ref[...] = _pack_halves(h_ref[...].astype(F32))

    def row_copy(t, k):
        return pltpu.make_async_copy(buf_ref.at[pl.ds(t, 1)],
                                     hs_ref.at[pl.ds(pos_ref[base + TOP_K * t + k], 1)], sem)

    def start(t, _):
        for k in range(TOP_K):
            row_copy(t, k).start(priority=k)
        return 0

    def wait(t, _):
        for k in range(TOP_K):
            row_copy(t, k).wait()
        return 0

    lax.fori_loop(0, tr, start, 0, unroll=TILES["dma_unroll"])
    lax.fori_loop(0, tr, wait, 0, unroll=TILES["dma_unroll"])


def _dispatch(h, pos, tile_rows, tm):
    n_rows = tile_rows.shape[0] * tm
    t, d = h.shape
    tr = min(TILES["dispatch_rows"], t)
    return pl.pallas_call(
        _dispatch_kernel,
        grid_spec=pltpu.PrefetchScalarGridSpec(
            num_scalar_prefetch=2, grid=(t // tr,),
            in_specs=[pl.BlockSpec((tr, d), lambda i, pos, lt: (i, 0))],
            out_specs=pl.BlockSpec(memory_space=pl.ANY),
            scratch_shapes=[pltpu.VMEM((tr, d // 2), U32), pltpu.VMEM((tm, d // 2), U32),
                            pltpu.SemaphoreType.DMA, pltpu.SemaphoreType.DMA]),
        out_shape=jax.ShapeDtypeStruct((n_rows, d // 2), U32),
        compiler_params=_cparams(1),
        name="moe_dispatch",
    )(pos, tile_rows, h)


def _combine_kernel(pos_ref, *refs, n_norms, emit_x):
    it = iter(refs)
    info_ref, x_ref, gate_ref = (next(it) for _ in range(3))
    norm_refs = [next(it) for _ in range(3 * n_norms)]
    ys_ref = next(it)
    xo_ref = next(it) if emit_x else None
    out_refs = [next(it) for _ in range(n_norms)]
    buf_ref, sems = next(it), next(it)
    tc = x_ref.shape[0]
    step, n_steps = pl.program_id(0), pl.num_programs(0)
    slot = step % 2

    def row_copy(s, sl, t, k):
        src = pos_ref[s * (TOP_K * tc) + TOP_K * t + k]
        return pltpu.make_async_copy(ys_ref.at[pl.ds(src, 1)], buf_ref.at[sl, k, pl.ds(t, 1)],
                                     sems.at[sl])

    def start_rows(s, sl):
        def body(t, _):
            for k in range(TOP_K):
                row_copy(s, sl, t, k).start(priority=k)
            return 0
        lax.fori_loop(0, tc, body, 0, unroll=TILES["dma_unroll"])

    def wait_rows(s, sl):
        def body(t, _):
            for k in range(TOP_K):
                row_copy(s, sl, t, k).wait()
            return 0
        lax.fori_loop(0, tc, body, 0, unroll=TILES["dma_unroll"])

    pl.when(step == 0)(lambda: start_rows(step, slot))
    pl.when(step + 1 < n_steps)(lambda: start_rows(step + 1, 1 - slot))
    wait_rows(step, slot)
    info = info_ref[...]
    w1, w2 = info[:, 2:3], info[:, 3:4]
    a_hi, a_lo = _unpack_halves(buf_ref[slot, 0])
    b_hi, b_lo = _unpack_halves(buf_ref[slot, 1])
    y = jnp.concatenate([w1 * a_hi + w2 * b_hi, w1 * a_lo + w2 * b_lo], axis=1)
    xn = x_ref[...] + gate_ref[0] * y
    if emit_x:
        xo_ref[...] = xn
    _norm_outputs(xn, norm_refs, out_refs)


def _combine(ys, pos, info, x, gate, norms, out_dtypes, seq, emit_x=True):
    t, d = x.shape
    tc = min(TILES["combine_rows"], seq)
    tpb = seq // tc
    row = pl.BlockSpec((tc, d), lambda i, pos: (i, 0))
    full = pl.BlockSpec((1, d), lambda i, pos: (0, 0))
    vec = pl.BlockSpec((1, 1, d), lambda i, pos: (i // tpb, 0, 0))
    in_specs = [pl.BlockSpec((tc, info.shape[1]), lambda i, pos: (i, 0)), row, vec]
    args = [info, x, gate]
    for g, sc, sh in norms:
        in_specs += [full, vec, vec]
        args += [g, sc, sh]
    in_specs += [pl.BlockSpec(memory_space=pl.ANY)]
    args += [ys]
    out_specs = [row] * (len(norms) + int(emit_x))
    out_shape = ([jax.ShapeDtypeStruct((t, d), F32)] if emit_x else []) + \
        [jax.ShapeDtypeStruct((t, d), dt) for dt in out_dtypes]
    return pl.pallas_call(
        functools.partial(_combine_kernel, n_norms=len(norms), emit_x=emit_x),
        grid_spec=pltpu.PrefetchScalarGridSpec(
            num_scalar_prefetch=1, grid=(t // tc,),
            in_specs=in_specs, out_specs=out_specs,
            scratch_shapes=[pltpu.VMEM((2, TOP_K, tc, d // 2), U32), pltpu.SemaphoreType.DMA((2,))]),
        out_shape=out_shape,
        compiler_params=_cparams(1),
        name="moe_combine",
    )(pos, *args)


def _moe_layer(h, x, gate, norms, out_dtypes, seq, router_w, router_b, wg, wu, wd, first_expert, emit_x=True):
    t, d = h.shape
    ne = router_w.shape[1]
    tm = min(TILES["moe_rows"], t)
    rw = jnp.zeros((d, LANES), BF16).at[:, :ne].set(router_w.astype(BF16))
    rb = jnp.zeros((1, LANES), F32).at[0, :ne].set(router_b)
    logits = _proj(h, rw, rb, F32, name="router_logits")
    info, counts = _route(logits, ne)
    counts = counts[0, :ne].astype(I32)
    padded = (counts + tm - 1) // tm * tm
    ends = jnp.cumsum(padded)
    starts = ends - padded
    n_tiles = (TOP_K * t) // tm + ne
    tile_row0 = jnp.arange(n_tiles, dtype=I32) * tm
    tile_e = jnp.minimum(jnp.sum(tile_row0[:, None] >= ends[None, :], axis=1), ne - 1).astype(I32)
    tile_rows = jnp.clip(starts[tile_e] + counts[tile_e] - tile_row0, 0, tm)
    tile_rows = jnp.where(tile_row0 < ends[-1], tile_rows, 0).astype(I32)
    last_e = jnp.max(jnp.where(counts > 0, jnp.arange(ne, dtype=I32), 0))
    tile_e = jnp.where(tile_row0 < ends[-1], tile_e, last_e).astype(I32)
    e_idx = info[:, 0:TOP_K].astype(I32)
    rank = info[:, 4:4 + TOP_K].astype(I32)
    start_of = jnp.sum(jnp.where(e_idx[:, :, None] == jnp.arange(ne, dtype=I32), starts, 0), axis=-1)
    pos = (start_of + rank).reshape(-1).astype(I32)
    hs = _dispatch(h, pos, tile_rows, tm)
    ys = _grouped_ffn(hs, wg, wu, wd, tile_e + first_expert, tile_rows, tm)
    return _combine(ys, pos, info, x, gate, norms, out_dtypes, seq, emit_x=emit_x)


def kernel(x, c, ada_w, ada_b, norm_mix_g, norm_ffn_g, conv_pw1_w, conv_pw1_b, conv_dw_w, conv_dw_b, conv_ln_g, conv_ln_b, conv_pw2_w, conv_pw2_b, kv_ada_w, kv_ada_b, kv_norm_g, w_kvf, b_f, attn_wq, attn_wo, ffn_w_gate, ffn_w_up, ffn_w_down, moe_router_w, moe_router_b, moe_w_gate, moe_w_up, moe_w_down, final_norm_g):
    nb, seq, d = x.shape
    depth = ada_w.shape[0]
    n_conv = conv_pw1_w.shape[0]
    nh = b_f.shape[0]
    t = nb * seq
    xs = x.reshape(t, d)

    mods = _ada(c, ada_w, ada_b)
    kv_mod = _ada(c, kv_ada_w[None], kv_ada_b[None])[0]

    def vec(v):
        return v.reshape(nb, 1, d)

    def mod(layer, k):
        return vec(mods[layer, :, k * d:(k + 1) * d])

    row = lambda v: v.reshape(1, d)
    zero_vec = jnp.zeros((nb, 1, d), F32)

    def mix_norm(layer):
        return (row(norm_mix_g[layer]), mod(layer, 1), mod(layer, 0))

    def ffn_norm(layer):
        return (row(norm_ffn_g[layer]), mod(layer, 4), mod(layer, 3))

    def after_ffn(layer):
        if layer == depth - 1:
            return [(row(final_norm_g), zero_vec, zero_vec)], [F32]
        norms, dts = [mix_norm(layer + 1)], [BF16]
        if layer + 1 == n_conv:
            norms.append((row(kv_norm_g), vec(kv_mod[:, d:]), vec(kv_mod[:, :d])))
            dts.append(BF16)
        return norms, dts

    ffn_tf = min(TILES["ffn_cols"], ffn_w_gate.shape[-1])
    ffn_g = _column_blocks(ffn_w_gate.astype(BF16), ffn_tf)
    ffn_u = _column_blocks(ffn_w_up.astype(BF16), ffn_tf)
    ffn_d = ffn_w_down.astype(BF16)
    h = _first_norm(xs, *mix_norm(0), seq)
    kv = cum_rows = cum_cols = None
    out = None
    for layer in range(depth):
        g1, g2 = mod(layer, 2), mod(layer, 5)
        if layer < n_conv:
            i = layer
            u = _glu_proj(h, conv_pw1_w[i].astype(BF16), conv_pw1_b[i].reshape(1, 2 * d))
            xs, h = _out_proj(u, conv_pw2_w[i].astype(BF16), row(conv_pw2_b[i]), xs, g1,
                              [ffn_norm(layer)], [BF16], seq,
                              conv=(conv_dw_w[i], row(conv_dw_b[i]), row(conv_ln_g[i]), row(conv_ln_b[i])),
                              name="conv_out_proj")
        else:
            i = layer - n_conv
            q = _proj(h, attn_wq[i].astype(BF16), jnp.zeros((1, d), F32), BF16,
                      scale=float(d // nh) ** -0.5 * LOG2E, name="attn_q_proj")
            o = _attention(q, kv, cum_rows, cum_cols, nb, seq, nh)
            xs, h = _out_proj(o, attn_wo[i].astype(BF16), jnp.zeros((1, d), F32), xs, g1,
                              [ffn_norm(layer)], [BF16], seq, name="attn_out_proj")
        norms, dts = after_ffn(layer)
        emit_x = layer != depth - 1
        j = layer // 2
        if layer % 2 == 0:
            outs = _dense_ffn(h, ffn_g, ffn_u, ffn_d, j, xs, g2, norms, dts, seq, emit_x=emit_x)
        else:
            ne, ff = moe_w_gate.shape[1], moe_w_gate.shape[3]
            outs = _moe_layer(h, xs, g2, norms, dts, seq, moe_router_w[j], moe_router_b[j],
                              moe_w_gate.reshape(-1, d, ff), moe_w_up.reshape(-1, d, ff),
                              moe_w_down.reshape(-1, ff, d), j * ne, emit_x=emit_x)
        if emit_x:
            xs, h = outs[0], outs[1]
        else:
            out = outs[0]
        if layer + 1 == n_conv:
            h_kv = outs[2]
            kv = _proj(h_kv, w_kvf[:, :2 * d].astype(BF16), jnp.zeros((1, 2 * d), F32), BF16,
                       name="shared_kv_proj")
            wf = jnp.zeros((d, LANES), BF16).at[:, :nh].set(w_kvf[:, 2 * d:].astype(BF16))
            fl = _proj(h_kv, wf, jnp.zeros((1, LANES), F32), F32, name="forget_logits")
            bf = jnp.zeros((1, LANES), F32).at[0, :nh].set(b_f)
            cum = _forget_cumsum(fl, bf, seq)[:, :nh].reshape(nb, seq, nh)
            cum_rows = cum.transpose(0, 2, 1).reshape(nb * nh, seq)
            cum_cols = cum_rows.reshape(nb * nh, seq, 1)
    return out.reshape(nb, seq, d)
```

```python
import functools

import jax
import jax.numpy as jnp
from jax import lax
from jax.experimental import pallas as pl
from jax.experimental.pallas import tpu as pltpu

F32 = jnp.float32
BF16 = jnp.bfloat16
U32 = jnp.uint32
I32 = jnp.int32

EPS = 1e-6
TOP_K = 2
LANES = 128
MXU_DIM = 256
HALO_ROWS = 32
HI_MASK = 0xFFFF0000
SUBLANES = 8
LOG2E = 1.4426950408889634

TILES = dict(
    norm_rows=512,
    mm_rows=1024, mm_cols=512,
    res_rows=256,
    attn_q=512, attn_keys=512, attn_heads=8,
    conv_rows=32, conv_lanes=512,
    dma_unroll=8,
    ffn_rows=512, ffn_cols=512,
    moe_rows=1024, moe_cols=256, moe_row_splits=3,
    route_rows=512,
    dispatch_rows=512,
    combine_rows=256,
    ada_cols=1024,
    cum_rows=256,
    epi_chunk=128,
)
VMEM_LIMIT = 56 * 2**20


def _cparams(n_axes, vmem=VMEM_LIMIT):
    return pltpu.CompilerParams(dimension_semantics=("arbitrary",) * n_axes,
                                vmem_limit_bytes=vmem)


def _sigmoid(v):
    return 1.0 / (1.0 + jnp.exp(-v))


def _silu(v):
    return v * _sigmoid(v)


def _pack_halves(v):
    n = v.shape[1] // 2
    hi = lax.bitcast_convert_type(v[:, :n].astype(BF16).astype(F32), U32)
    lo = lax.bitcast_convert_type(v[:, n:].astype(BF16).astype(F32), U32)
    return hi | lax.shift_right_logical(lo, jnp.uint32(16))


def _unpack_halves(p):
    hi = lax.bitcast_convert_type(p & jnp.uint32(HI_MASK), F32)
    lo = lax.bitcast_convert_type(lax.shift_left(p, jnp.uint32(16)), F32)
    return hi, lo


def _norm_outputs(xn, norm_refs, out_refs):
    ms = jnp.mean(xn * xn, axis=-1, keepdims=True)
    xr = xn * lax.rsqrt(ms + EPS)
    for k, o_ref in enumerate(out_refs):
        g = norm_refs[3 * k][...]
        sc = norm_refs[3 * k + 1][0]
        sh = norm_refs[3 * k + 2][0]
        o_ref[...] = (xr * g * (1.0 + sc) + sh).astype(o_ref.dtype)


def _ada_kernel(c_ref, w_ref, b_ref, o_ref):
    ca = _silu(c_ref[...]).astype(BF16)
    o_ref[0] = jnp.dot(ca, w_ref[0].astype(BF16), preferred_element_type=F32) + b_ref[0]


def _ada(c, w, b):
    nl, d, n = w.shape
    nb = c.shape[0]
    tn = min(TILES["ada_cols"], n)
    return pl.pallas_call(
        _ada_kernel,
        grid=(nl, n // tn),
        in_specs=[pl.BlockSpec((nb, d), lambda l, j: (0, 0)),
                  pl.BlockSpec((1, d, tn), lambda l, j: (l, 0, j)),
                  pl.BlockSpec((1, 1, tn), lambda l, j: (l, 0, j))],
        out_specs=pl.BlockSpec((1, nb, tn), lambda l, j: (l, 0, j)),
        out_shape=jax.ShapeDtypeStruct((nl, nb, n), F32),
        compiler_params=_cparams(2),
        name="ada_modulation",
    )(c, w, b.reshape(nl, 1, n))


def _norm_kernel(x_ref, g_ref, sc_ref, sh_ref, o_ref):
    _norm_outputs(x_ref[...], (g_ref, sc_ref, sh_ref), (o_ref,))


def _vec_spec(d, rows_per_batch_tile):
    return pl.BlockSpec((1, 1, d), lambda i, *_: (i // rows_per_batch_tile, 0, 0))


def _first_norm(x, g, sc, sh, seq):
    t, d = x.shape
    tm = min(TILES["norm_rows"], seq)
    tpb = seq // tm
    return pl.pallas_call(
        _norm_kernel,
        grid=(t // tm,),
        in_specs=[pl.BlockSpec((tm, d), lambda i: (i, 0)),
                  pl.BlockSpec((1, d), lambda i: (0, 0)),
                  _vec_spec(d, tpb), _vec_spec(d, tpb)],
        out_specs=pl.BlockSpec((tm, d), lambda i: (i, 0)),
        out_shape=jax.ShapeDtypeStruct((t, d), BF16),
        compiler_params=_cparams(1),
        name="first_norm",
    )(x, g, sc, sh)


def _proj_kernel(a_ref, w_ref, b_ref, o_ref, *, scale):
    y = jnp.dot(a_ref[...], w_ref[...], preferred_element_type=F32) + b_ref[...]
    if scale != 1.0:
        y = y * scale
    o_ref[...] = y.astype(o_ref.dtype)


def _proj(a, w, b, out_dtype, scale=1.0, name="proj"):
    t, d = a.shape
    n = w.shape[1]
    tm = min(TILES["mm_rows"], t)
    tn = min(TILES["mm_cols"], n)
    return pl.pallas_call(
        functools.partial(_proj_kernel, scale=scale),
        grid=(t // tm, n // tn),
        in_specs=[pl.BlockSpec((tm, d), lambda i, j: (i, 0)),
                  pl.BlockSpec((d, tn), lambda i, j: (0, j)),
                  pl.BlockSpec((1, tn), lambda i, j: (0, j))],
        out_specs=pl.BlockSpec((tm, tn), lambda i, j: (i, j)),
        out_shape=jax.ShapeDtypeStruct((t, n), out_dtype),
        compiler_params=_cparams(2),
        name=name,
    )(a, w, b)


def _glu_kernel(a_ref, wa_ref, wb_ref, ba_ref, bb_ref, o_ref):
    a = a_ref[...]
    lin = jnp.dot(a, wa_ref[...], preferred_element_type=F32) + ba_ref[...]
    gate = jnp.dot(a, wb_ref[...], preferred_element_type=F32) + bb_ref[...]
    o_ref[...] = (lin * _sigmoid(gate)).astype(o_ref.dtype)


def _glu_proj(a, w, b):
    t, d = a.shape
    tm = min(TILES["mm_rows"], t)
    tn = min(TILES["mm_cols"], d)
    nj = d // tn
    return pl.pallas_call(
        _glu_kernel,
        grid=(t // tm, nj),
        in_specs=[pl.BlockSpec((tm, d), lambda i, j: (i, 0)),
                  pl.BlockSpec((d, tn), lambda i, j: (0, j)),
                  pl.BlockSpec((d, tn), lambda i, j: (0, j + nj)),
                  pl.BlockSpec((1, tn), lambda i, j: (0, j)),
                  pl.BlockSpec((1, tn), lambda i, j: (0, j + nj))],
        out_specs=pl.BlockSpec((tm, tn), lambda i, j: (i, j)),
        out_shape=jax.ShapeDtypeStruct((t, d), BF16),
        compiler_params=_cparams(2),
        name="conv_pw1_glu",
    )(a, w, w, b, b)


def _causal_conv_ln_swish(u_ref, halo_ref, dw_ref, dwb_ref, lng_ref, lnb_ref,
                          win_ref, shift_ref, taps_ref, conv_ref, first_in_seq, width):
    tm, d = u_ref.shape
    n = tm + HALO_ROWS
    halo = halo_ref[...].astype(F32)
    win_ref[0:HALO_ROWS, :] = jnp.where(first_in_seq, 0.0, halo)
    win_ref[HALO_ROWS:, :] = u_ref[...].astype(F32)
    base = HALO_ROWS - (width - 1)
    rc = min(TILES["conv_rows"], tm)
    lc = shift_ref.shape[2]
    for c0 in range(0, d, lc):
        cols = slice(c0, c0 + lc)
        for b in range(1, SUBLANES):
            shift_ref[b - 1, 0:n - SUBLANES, :] = win_ref[b:n - SUBLANES + b, cols]

        for k in range(width):
            taps_ref[k] = jnp.broadcast_to(dw_ref[k:k + 1, cols], (SUBLANES, lc))
        taps_ref[width] = jnp.broadcast_to(dwb_ref[:, cols], (SUBLANES, lc))

        def rows_chunk(ci, _, cols=cols):
            r0 = ci * rc
            groups = range(rc // SUBLANES)
            acc = [taps_ref[width] for _ in groups]
            for k in range(width):
                blk, b = divmod(base + k, SUBLANES)
                w = taps_ref[k]
                for g in groups:
                    rs = pl.ds(pl.multiple_of(r0 + (blk + g) * SUBLANES, SUBLANES), SUBLANES)
                    src = win_ref[rs, cols] if b == 0 else shift_ref[b - 1, rs, :]
                    acc[g] = acc[g] + src * w
            for g in groups:
                conv_ref[pl.ds(pl.multiple_of(r0 + g * SUBLANES, SUBLANES), SUBLANES), cols] = acc[g]
            return 0

        lax.fori_loop(0, tm // rc, rows_chunk, 0)
    v = conv_ref[...]
    mu = jnp.mean(v, axis=-1, keepdims=True)
    vc = v - mu
    var = jnp.mean(vc * vc, axis=-1, keepdims=True)
    y = vc * lax.rsqrt(var + EPS) * lng_ref[...] + lnb_ref[...]
    return _silu(y).astype(BF16)


def _out_proj_kernel(*refs, conv_width, n_norms, emit_x, rows_per_seq_tile):
    it = iter(refs)
    if conv_width:
        u_ref, halo_ref, dw_ref, dwb_ref, lng_ref, lnb_ref = (next(it) for _ in range(6))
    else:
        a_ref = next(it)
    w_ref, b_ref, x_ref, gate_ref = (next(it) for _ in range(4))
    norm_refs = [next(it) for _ in range(3 * n_norms)]
    xo_ref = next(it) if emit_x else None
    out_refs = [next(it) for _ in range(n_norms)]
    if conv_width:
        win_ref, shift_ref, taps_ref, conv_ref = (next(it) for _ in range(4))
        first = (pl.program_id(0) % rows_per_seq_tile) == 0
        a = _causal_conv_ln_swish(u_ref, halo_ref, dw_ref, dwb_ref, lng_ref, lnb_ref,
                                  win_ref, shift_ref, taps_ref, conv_ref, first, conv_width)
    else:
        a = a_ref[...]
    y = jnp.dot(a, w_ref[...], preferred_element_type=F32) + b_ref[...]
    xn = x_ref[...] + gate_ref[0] * y
    if emit_x:
        xo_ref[...] = xn
    _norm_outputs(xn, norm_refs, out_refs)


def _out_proj(a, w, b, x, gate, norms, out_dtypes, seq, conv=None, emit_x=True, name="out_proj"):
    t, d = x.shape
    tm = min(TILES["res_rows"], seq)
    tpb = seq // tm
    row = pl.BlockSpec((tm, d), lambda i: (i, 0))
    full = lambda r, c: pl.BlockSpec((r, c), lambda i: (0, 0))
    in_specs, args, scratch = [], [], []
    width = 0
    if conv is not None:
        dw, dwb, lng, lnb = conv
        width = dw.shape[0]
        assert width - 1 <= HALO_ROWS and tm % HALO_ROWS == 0
        hpt = tm // HALO_ROWS
        in_specs += [row,
                     pl.BlockSpec((HALO_ROWS, d), lambda i: (jnp.maximum(i * hpt - 1, 0), 0)),
                     full(width, d), full(1, d), full(1, d), full(1, d)]
        args += [a, a, dw, dwb, lng, lnb]
        lc = min(TILES["conv_lanes"], d)
        scratch = [pltpu.VMEM((tm + HALO_ROWS, d), F32),
                   pltpu.VMEM((SUBLANES - 1, tm + HALO_ROWS, lc), F32),
                   pltpu.VMEM((width + 1, SUBLANES, lc), F32),
                   pltpu.VMEM((tm, d), F32)]
    else:
        in_specs += [row]
        args += [a]
    in_specs += [full(d, d), full(1, d), row, _vec_spec(d, tpb)]
    args += [w, b, x, gate]
    for g, sc, sh in norms:
        in_specs += [full(1, d), _vec_spec(d, tpb), _vec_spec(d, tpb)]
        args += [g, sc, sh]
    out_specs = [row] * (len(norms) + int(emit_x))
    out_shape = ([jax.ShapeDtypeStruct((t, d), F32)] if emit_x else []) + \
        [jax.ShapeDtypeStruct((t, d), dt) for dt in out_dtypes]
    return pl.pallas_call(
        functools.partial(_out_proj_kernel, conv_width=width, n_norms=len(norms),
                          emit_x=emit_x, rows_per_seq_tile=tpb),
        grid=(t // tm,),
        in_specs=in_specs, out_specs=out_specs, out_shape=out_shape,
        scratch_shapes=scratch,
        compiler_params=_cparams(1),
        name=name,
    )(*args)


def _cum_kernel(fl_ref, bf_ref, o_ref, *, blk):
    s = fl_ref.shape[0]
    r = lax.broadcasted_iota(I32, (blk, blk), 0)
    c = lax.broadcasted_iota(I32, (blk, blk), 1)
    tri = (r >= c).astype(BF16)
    carry = jnp.zeros((1, fl_ref.shape[1]), F32)
    for b0 in range(0, s, blk):
        z = fl_ref[b0:b0 + blk, :] + bf_ref[...]
        lf = jnp.minimum(z, 0.0) - jnp.log(1.0 + jnp.exp(-jnp.abs(z)))
        p0 = lf.astype(BF16)
        r1 = lf - p0.astype(F32)
        p1 = r1.astype(BF16)
        p2 = (r1 - p1.astype(F32)).astype(BF16)
        cs = (jnp.dot(tri, p0, preferred_element_type=F32)
              + jnp.dot(tri, p1, preferred_element_type=F32)
              + jnp.dot(tri, p2, preferred_element_type=F32)) + carry
        o_ref[b0:b0 + blk, :] = cs * LOG2E
        carry = cs[blk - 1:blk, :]


def _forget_cumsum(fl, bf, seq):
    t, n = fl.shape
    blk = min(TILES["cum_rows"], seq)
    return pl.pallas_call(
        functools.partial(_cum_kernel, blk=blk),
        grid=(t // seq,),
        in_specs=[pl.BlockSpec((seq, n), lambda b: (b, 0)),
                  pl.BlockSpec((1, n), lambda b: (0, 0))],
        out_specs=pl.BlockSpec((seq, n), lambda b: (b, 0)),
        out_shape=jax.ShapeDtypeStruct((t, n), F32),
        compiler_params=_cparams(1),
        name="forget_cumsum",
    )(fl, bf)


def _attn_kernel(q_ref, k_ref, v_ref, fs_ref, ft_ref, o_ref,
                 fsb_ref, vt_ref, m_ref, l_ref, acc_ref, *, tq, tk):
    i = pl.program_id(2)
    assert q_ref.shape[1] == fs_ref.shape[0] * LANES
    blocks_per_tile = tq // tk
    nt = (((1,), (1,)), ((), ()))

    def lanes(v, n):
        return v if n == LANES else jnp.concatenate([v] * (n // LANES), axis=1)

    heads = range(fs_ref.shape[0])
    head_lanes = [slice(h * LANES, (h + 1) * LANES) for h in heads]

    @pl.when(i == 0)
    def _():
        for h in heads:
            fsb_ref[h] = jnp.broadcast_to(fs_ref[h], fsb_ref.shape[1:])
            for j in range(vt_ref.shape[1]):
                vt_ref[h, j] = v_ref[j * tk:(j + 1) * tk, head_lanes[h]].astype(F32).T.astype(BF16)

    m_ref[...] = jnp.full(m_ref.shape, -jnp.inf, F32)
    l_ref[...] = jnp.zeros(l_ref.shape, F32)
    acc_ref[...] = jnp.zeros(acc_ref.shape, F32)

    def update(h, j, q0, causal):
        nq = tq - q0
        k0 = pl.multiple_of(j * tk, tk)
        cols = slice(q0, tq)
        s = lax.dot_general(k_ref[pl.ds(k0, tk), head_lanes[h]], q_ref[cols, head_lanes[h]], nt,
                            preferred_element_type=F32)
        s = s + ft_ref[h, :, cols] - lanes(fsb_ref[h, pl.ds(k0, tk), :], nq)
        if causal:
            r = lax.broadcasted_iota(I32, s.shape, 0)
            c = lax.broadcasted_iota(I32, s.shape, 1)
            s = jnp.where(c >= r, s, -jnp.inf)
        m = m_ref[h, :, cols]
        m_new = jnp.maximum(m, jnp.max(s, axis=0, keepdims=True))
        p = jnp.exp2(s - m_new)
        alpha = jnp.exp2(m - m_new)
        m_ref[h, :, cols] = m_new
        l_ref[h, :, cols] = alpha * l_ref[h, :, cols] + jnp.sum(p, axis=0, keepdims=True)
        acc_ref[h, :, cols] = alpha * acc_ref[h, :, cols] + jnp.dot(
            vt_ref[h, j], p.astype(BF16), preferred_element_type=F32)

    def key_tile(jt, _):
        for kb in range(blocks_per_tile):
            for h in heads:
                update(h, jt * blocks_per_tile + kb, 0, False)
        return 0

    lax.fori_loop(0, i, key_tile, 0)
    for kb in range(blocks_per_tile):
        for h in heads:
            update(h, i * blocks_per_tile + kb, kb * tk, True)
    for h in heads:
        o_ref[:, head_lanes[h]] = (acc_ref[h] / l_ref[h]).T.astype(o_ref.dtype)


def _attention(q, kv, cum_rows, cum_cols, nb, seq, nh):
    t, d = q.shape
    dh = d // nh
    tq = min(TILES["attn_q"], seq)
    tk = min(TILES["attn_keys"], tq)
    hg = min(TILES["attn_heads"], nh)
    nq = seq // tq
    ng = nh // hg
    return pl.pallas_call(
        functools.partial(_attn_kernel, tq=tq, tk=tk),
        grid=(nb, ng, nq),
        in_specs=[pl.BlockSpec((tq, hg * dh), lambda b, g, i: (b * nq + i, g)),
                  pl.BlockSpec((seq, hg * dh), lambda b, g, i: (b, g)),
                  pl.BlockSpec((seq, hg * dh), lambda b, g, i: (b, ng + g)),
                  pl.BlockSpec((hg, seq, 1), lambda b, g, i: (b * ng + g, 0, 0)),
                  pl.BlockSpec((hg, 1, tq), lambda b, g, i: (b * ng + g, 0, i))],
        out_specs=pl.BlockSpec((tq, hg * dh), lambda b, g, i: (b * nq + i, g)),
        out_shape=jax.ShapeDtypeStruct((t, d), BF16),
        scratch_shapes=[pltpu.VMEM((hg, seq, LANES), F32), pltpu.VMEM((hg, seq // tk, dh, tk), BF16),
                        pltpu.VMEM((hg, 1, tq), F32), pltpu.VMEM((hg, 1, tq), F32),
                        pltpu.VMEM((hg, dh, tq), F32)],
        compiler_params=_cparams(3),
        name="forgetting_attention",
    )(q, kv, kv, cum_cols, cum_rows.reshape(nb * nh, 1, seq))


def _ffn_kernel(te_ref, nv_ref, *refs, grouped, n_norms, emit_x, cast_w, chunk):
    it = iter(refs)
    h_ref, wg_ref, wu_ref, wd_ref = (next(it) for _ in range(4))
    if not grouped:
        x_ref, gate_ref = next(it), next(it)
        norm_refs = [next(it) for _ in range(3 * n_norms)]
        xo_ref = next(it) if emit_x else None
        out_refs = [next(it) for _ in range(n_norms)]
    else:
        y_ref = next(it)
    acc_ref = next(it)
    hb_ref = next(it) if grouped else h_ref
    i, f = pl.program_id(0), pl.program_id(1)
    nf = pl.num_programs(1)
    tm = acc_ref.shape[0]

    @pl.when(f == 0)
    def _():
        acc_ref[...] = jnp.zeros_like(acc_ref)
        if grouped:
            hi, lo = _unpack_halves(h_ref[...])
            hb_ref[...] = jnp.concatenate([hi, lo], axis=1).astype(BF16)

    def step(rows):
        h = hb_ref[0:rows, :]
        wg, wu, wd = wg_ref[0], wu_ref[0], wd_ref[0]
        if cast_w:
            wg, wu, wd = wg.astype(BF16), wu.astype(BF16), wd.astype(BF16)
        g = jnp.dot(h, wg, preferred_element_type=F32)
        u = jnp.dot(h, wu, preferred_element_type=F32)
        a = (_silu(g) * u).astype(BF16)
        acc_ref[0:rows, :] += jnp.dot(a, wd, preferred_element_type=F32)

    if grouped:
        nv = nv_ref[i]
        sizes = [tm // 2 ** k for k in range(TILES["moe_row_splits"])]
        for k, rows in enumerate(sizes):
            below = sizes[k + 1] if k + 1 < len(sizes) else 0
            cond = nv > below if k == 0 else (nv > below) & (nv <= rows)
            pl.when(cond)(functools.partial(step, rows))
    else:
        step(tm)

    @pl.when(f == nf - 1)
    def _():
        if grouped:
            y_ref[...] = _pack_halves(acc_ref[...])
        else:
            def rows_chunk(c, _):
                r0 = pl.multiple_of(c * chunk, chunk)
                rs = pl.ds(r0, chunk)
                xn = x_ref[rs, :] + gate_ref[0] * acc_ref[rs, :]
                if emit_x:
                    xo_ref[rs, :] = xn
                _norm_outputs(xn, norm_refs, [o.at[rs, :] for o in out_refs])
                return 0
            lax.fori_loop(0, tm // chunk, rows_chunk, 0)


def _ffn_weight_specs(d, ff, tf, nf):
    def col(i, f, te, nv):
        return (te[i], 0, jnp.where(nv[i] > 0, f, nf - 1))

    def row(i, f, te, nv):
        return (te[i], jnp.where(nv[i] > 0, f, nf - 1), 0)
    return [pl.BlockSpec((1, d, tf), col), pl.BlockSpec((1, d, tf), col), pl.BlockSpec((1, tf, d), row)]


def _dense_ffn(h, wg, wu, wd, which, x, gate, norms, out_dtypes, seq, emit_x=True):
    t, d = x.shape
    ff = wg.shape[-1]
    tm = min(TILES["ffn_rows"], seq)
    tf = min(TILES["ffn_cols"], ff)
    nf = ff // tf
    tpb = seq // tm
    nt = t // tm
    row = pl.BlockSpec((tm, d), lambda i, f, te, nv: (i, 0))
    full = pl.BlockSpec((1, d), lambda i, f, te, nv: (0, 0))
    vec = pl.BlockSpec((1, 1, d), lambda i, f, te, nv: (i // tpb, 0, 0))
    in_specs = [row] + _ffn_weight_specs(d, ff, tf, nf) + [row, vec]
    args = [h, wg, wu, wd, x, gate]
    for g, sc, sh in norms:
        in_specs += [full, vec, vec]
        args += [g, sc, sh]
    out_specs = [row] * (len(norms) + int(emit_x))
    out_shape = ([jax.ShapeDtypeStruct((t, d), F32)] if emit_x else []) + \
        [jax.ShapeDtypeStruct((t, d), dt) for dt in out_dtypes]
    chunk = min(TILES["epi_chunk"], tm)
    return pl.pallas_call(
        functools.partial(_ffn_kernel, grouped=False, n_norms=len(norms), emit_x=emit_x,
                          cast_w=wg.dtype != BF16, chunk=chunk),
        grid_spec=pltpu.PrefetchScalarGridSpec(
            num_scalar_prefetch=2, grid=(nt, nf),
            in_specs=in_specs, out_specs=out_specs,
            scratch_shapes=[pltpu.VMEM((tm, d), F32)]),
        out_shape=out_shape,
        compiler_params=_cparams(2),
        name="dense_swiglu",
    )(jnp.full((nt,), which, I32), jnp.ones((nt,), I32), *args)


def _grouped_ffn(hs, wg, wu, wd, tile_expert, tile_rows, tm):
    r, dp = hs.shape
    d = 2 * dp
    ff = wg.shape[-1]
    tf = min(TILES["moe_cols"], ff)
    nf = ff // tf
    row = pl.BlockSpec((tm, dp), lambda i, f, te, nv: (i, 0))
    return pl.pallas_call(
        functools.partial(_ffn_kernel, grouped=True, n_norms=0, emit_x=False,
                          cast_w=wg.dtype != BF16, chunk=0),
        grid_spec=pltpu.PrefetchScalarGridSpec(
            num_scalar_prefetch=2, grid=(r // tm, nf),
            in_specs=[row] + _ffn_weight_specs(d, ff, tf, nf),
            out_specs=row,
            scratch_shapes=[pltpu.VMEM((tm, d), F32), pltpu.VMEM((tm, d), BF16)]),
        out_shape=jax.ShapeDtypeStruct((r, dp), U32),
        compiler_params=_cparams(2),
        name="expert_swiglu",
    )(tile_expert, tile_rows, hs, wg, wu, wd)


def _route_kernel(lg_ref, info_ref, cnt_ref, carry_ref, *, n_experts):
    tr, n = lg_ref.shape

    @pl.when(pl.program_id(0) == 0)
    def _():
        carry_ref[...] = jnp.zeros_like(carry_ref)

    lane = lax.broadcasted_iota(I32, (tr, n), 1).astype(F32)
    lg = jnp.where(lane < n_experts, lg_ref[...], -jnp.inf)
    v1 = jnp.max(lg, axis=-1, keepdims=True)
    i1 = jnp.min(jnp.where(lg == v1, lane, float(n)), axis=-1, keepdims=True)
    lg2 = jnp.where(lane == i1, -jnp.inf, lg)
    v2 = jnp.max(lg2, axis=-1, keepdims=True)
    i2 = jnp.min(jnp.where(lg2 == v2, lane, float(n)), axis=-1, keepdims=True)
    e = jnp.exp(v2 - v1)
    w1 = 1.0 / (1.0 + e)
    w2 = e / (1.0 + e)
    hot1 = lane == i1
    hot2 = lane == i2
    hot = jnp.where(hot1 | hot2, 1.0, 0.0)
    r = lax.broadcasted_iota(I32, (tr, tr), 0)
    c = lax.broadcasted_iota(I32, (tr, tr), 1)
    before = (r > c).astype(BF16)
    seen = jnp.dot(before, hot.astype(BF16), preferred_element_type=F32) + carry_ref[...]
    r1 = jnp.sum(jnp.where(hot1, seen, 0.0), axis=-1, keepdims=True)
    r2 = jnp.sum(jnp.where(hot2, seen, 0.0), axis=-1, keepdims=True)
    carry_ref[...] += jnp.sum(hot, axis=0, keepdims=True)
    cnt_ref[...] = carry_ref[...]
    cols = (i1, i2, w1, w2, r1, r2)
    out = jnp.zeros((tr, n), F32)
    for k, col in enumerate(cols):
        out = jnp.where(lane == float(k), col, out)
    info_ref[...] = out


def _route(logits, n_experts):
    t, n = logits.shape
    tr = min(TILES["route_rows"], t)
    return pl.pallas_call(
        functools.partial(_route_kernel, n_experts=n_experts),
        grid=(t // tr,),
        in_specs=[pl.BlockSpec((tr, n), lambda i: (i, 0))],
        out_specs=[pl.BlockSpec((tr, n), lambda i: (i, 0)), pl.BlockSpec((1, n), lambda i: (0, 0))],
        out_shape=[jax.ShapeDtypeStruct((t, n), F32), jax.ShapeDtypeStruct((1, n), F32)],
        scratch_shapes=[pltpu.VMEM((1, n), F32)],
        compiler_params=_cparams(1),
        name="route_top2",
    )(logits)


def _dispatch_kernel(pos_ref, tile_rows_ref, h_ref, hs_ref, buf_ref, zero_ref, sem, zsem):
    tr = h_ref.shape[0]
    base = pl.program_id(0) * (TOP_K * tr)

    @pl.when(pl.program_id(0) == 0)
    def _():
        zero_ref[...] = jnp.zeros_like(zero_ref)
        tm = zero_ref.shape[0]

        def fill(i):
            return pltpu.make_async_copy(zero_ref, hs_ref.at[pl.ds(i * tm, tm)], zsem)

        for i in range(tile_rows_ref.shape[0]):
            pl.when(tile_rows_ref[i] < tm)(lambda i=i: fill(i).start())
        for i in range(tile_rows_ref.shape[0]):
            pl.when(tile_rows_ref[i] < tm)(lambda i=i: fill(i).wait())

    buf_ref[...] = _pack_halves(h_ref[...].astype(F32))

    def row_copy(t, k):
        return pltpu.make_async_copy(buf_ref.at[pl.ds(t, 1)],
                                     hs_ref.at[pl.ds(pos_ref[base + TOP_K * t + k], 1)], sem)

    def start(t, _):
        for k in range(TOP_K):
            row_copy(t, k).start(priority=k)
        return 0

    def wait(t, _):
        for k in range(TOP_K):
            row_copy(t, k).wait()
        return 0

    lax.fori_loop(0, tr, start, 0, unroll=TILES["dma_unroll"])
    lax.fori_loop(0, tr, wait, 0, unroll=TILES["dma_unroll"])


def _dispatch(h, pos, tile_rows, tm):
    n_rows = tile_rows.shape[0] * tm
    t, d = h.shape
    tr = min(TILES["dispatch_rows"], t)
    return pl.pallas_call(
        _dispatch_kernel,
        grid_spec=pltpu.PrefetchScalarGridSpec(
            num_scalar_prefetch=2, grid=(t // tr,),
            in_specs=[pl.BlockSpec((tr, d), lambda i, pos, lt: (i, 0))],
            out_specs=pl.BlockSpec(memory_space=pl.ANY),
            scratch_shapes=[pltpu.VMEM((tr, d // 2), U32), pltpu.VMEM((tm, d // 2), U32),
                            pltpu.SemaphoreType.DMA, pltpu.SemaphoreType.DMA]),
        out_shape=jax.ShapeDtypeStruct((n_rows, d // 2), U32),
        compiler_params=_cparams(1),
        name="moe_dispatch",
    )(pos, tile_rows, h)


def _combine_kernel(pos_ref, *refs, n_norms, emit_x):
    it = iter(refs)
    info_ref, x_ref, gate_ref = (next(it) for _ in range(3))
    norm_refs = [next(it) for _ in range(3 * n_norms)]
    ys_ref = next(it)
    xo_ref = next(it) if emit_x else None
    out_refs = [next(it) for _ in range(n_norms)]
    buf_ref, sems = next(it), next(it)
    tc = x_ref.shape[0]
    step, n_steps = pl.program_id(0), pl.num_programs(0)
    slot = step % 2

    def row_copy(s, sl, t, k):
        src = pos_ref[s * (TOP_K * tc) + TOP_K * t + k]
        return pltpu.make_async_copy(ys_ref.at[pl.ds(src, 1)], buf_ref.at[sl, k, pl.ds(t, 1)],
                                     sems.at[sl])

    def start_rows(s, sl):
        def body(t, _):
            for k in range(TOP_K):
                row_copy(s, sl, t, k).start(priority=k)
            return 0
        lax.fori_loop(0, tc, body, 0, unroll=TILES["dma_unroll"])

    def wait_rows(s, sl):
        def body(t, _):
            for k in range(TOP_K):
                row_copy(s, sl, t, k).wait()
            return 0
        lax.fori_loop(0, tc, body, 0, unroll=TILES["dma_unroll"])

    pl.when(step == 0)(lambda: start_rows(step, slot))
    pl.when(step + 1 < n_steps)(lambda: start_rows(step + 1, 1 - slot))
    wait_rows(step, slot)
    info = info_ref[...]
    w1, w2 = info[:, 2:3], info[:, 3:4]
    a_hi, a_lo = _unpack_halves(buf_ref[slot, 0])
    b_hi, b_lo = _unpack_halves(buf_ref[slot, 1])
    y = jnp.concatenate([w1 * a_hi + w2 * b_hi, w1 * a_lo + w2 * b_lo], axis=1)
    xn = x_ref[...] + gate_ref[0] * y
    if emit_x:
        xo_ref[...] = xn
    _norm_outputs(xn, norm_refs, out_refs)


def _combine(ys, pos, info, x, gate, norms, out_dtypes, seq, emit_x=True):
    t, d = x.shape
    tc = min(TILES["combine_rows"], seq)
    tpb = seq // tc
    row = pl.BlockSpec((tc, d), lambda i, pos: (i, 0))
    full = pl.BlockSpec((1, d), lambda i, pos: (0, 0))
    vec = pl.BlockSpec((1, 1, d), lambda i, pos: (i // tpb, 0, 0))
    in_specs = [pl.BlockSpec((tc, info.shape[1]), lambda i, pos: (i, 0)), row, vec]
    args = [info, x, gate]
    for g, sc, sh in norms:
        in_specs += [full, vec, vec]
        args += [g, sc, sh]
    in_specs += [pl.BlockSpec(memory_space=pl.ANY)]
    args += [ys]
    out_specs = [row] * (len(norms) + int(emit_x))
    out_shape = ([jax.ShapeDtypeStruct((t, d), F32)] if emit_x else []) + \
        [jax.ShapeDtypeStruct((t, d), dt) for dt in out_dtypes]
    return pl.pallas_call(
        functools.partial(_combine_kernel, n_norms=len(norms), emit_x=emit_x),
        grid_spec=pltpu.PrefetchScalarGridSpec(
            num_scalar_prefetch=1, grid=(t // tc,),
            in_specs=in_specs, out_specs=out_specs,
            scratch_shapes=[pltpu.VMEM((2, TOP_K, tc, d // 2), U32), pltpu.SemaphoreType.DMA((2,))]),
        out_shape=out_shape,
        compiler_params=_cparams(1),
        name="moe_combine",
    )(pos, *args)


def _moe_layer(h, x, gate, norms, out_dtypes, seq, router_w, router_b, wg, wu, wd, first_expert, emit_x=True):
    t, d = h.shape
    ne = router_w.shape[1]
    tm = min(TILES["moe_rows"], t)
    rw = jnp.zeros((d, LANES), BF16).at[:, :ne].set(router_w.astype(BF16))
    rb = jnp.zeros((1, LANES), F32).at[0, :ne].set(router_b)
    logits = _proj(h, rw, rb, F32, name="router_logits")
    info, counts = _route(logits, ne)
    counts = counts[0, :ne].astype(I32)
    padded = (counts + tm - 1) // tm * tm
    ends = jnp.cumsum(padded)
    starts = ends - padded
    n_tiles = (TOP_K * t) // tm + ne
    tile_row0 = jnp.arange(n_tiles, dtype=I32) * tm
    tile_e = jnp.minimum(jnp.sum(tile_row0[:, None] >= ends[None, :], axis=1), ne - 1).astype(I32)
    tile_rows = jnp.clip(starts[tile_e] + counts[tile_e] - tile_row0, 0, tm)
    tile_rows = jnp.where(tile_row0 < ends[-1], tile_rows, 0).astype(I32)
    last_e = jnp.max(jnp.where(counts > 0, jnp.arange(ne, dtype=I32), 0))
    tile_e = jnp.where(tile_row0 < ends[-1], tile_e, last_e).astype(I32)
    e_idx = info[:, 0:TOP_K].astype(I32)
    rank = info[:, 4:4 + TOP_K].astype(I32)
    start_of = jnp.sum(jnp.where(e_idx[:, :, None] == jnp.arange(ne, dtype=I32), starts, 0), axis=-1)
    pos = (start_of + rank).reshape(-1).astype(I32)
    hs = _dispatch(h, pos, tile_rows, tm)
    ys = _grouped_ffn(hs, wg, wu, wd, tile_e + first_expert, tile_rows, tm)
    return _combine(ys, pos, info, x, gate, norms, out_dtypes, seq, emit_x=emit_x)


def kernel(x, c, ada_w, ada_b, norm_mix_g, norm_ffn_g, conv_pw1_w, conv_pw1_b, conv_dw_w, conv_dw_b, conv_ln_g, conv_ln_b, conv_pw2_w, conv_pw2_b, kv_ada_w, kv_ada_b, kv_norm_g, w_kvf, b_f, attn_wq, attn_wo, ffn_w_gate, ffn_w_up, ffn_w_down, moe_router_w, moe_router_b, moe_w_gate, moe_w_up, moe_w_down, final_norm_g):
    nb, seq, d = x.shape
    depth = ada_w.shape[0]
    n_conv = conv_pw1_w.shape[0]
    nh = b_f.shape[0]
    t = nb * seq
    xs = x.reshape(t, d)

    mods = _ada(c, ada_w, ada_b)
    kv_mod = _ada(c, kv_ada_w[None], kv_ada_b[None])[0]

    def vec(v):
        return v.reshape(nb, 1, d)

    def mod(layer, k):
        return vec(mods[layer, :, k * d:(k + 1) * d])

    row = lambda v: v.reshape(1, d)
    zero_vec = jnp.zeros((nb, 1, d), F32)

    def mix_norm(layer):
        return (row(norm_mix_g[layer]), mod(layer, 1), mod(layer, 0))

    def ffn_norm(layer):
        return (row(norm_ffn_g[layer]), mod(layer, 4), mod(layer, 3))

    def after_ffn(layer):
        if layer == depth - 1:
            return [(row(final_norm_g), zero_vec, zero_vec)], [F32]
        norms, dts = [mix_norm(layer + 1)], [BF16]
        if layer + 1 == n_conv:
            norms.append((row(kv_norm_g), vec(kv_mod[:, d:]), vec(kv_mod[:, :d])))
            dts.append(BF16)
        return norms, dts

    ffn_g, ffn_u, ffn_d = (w.astype(BF16) for w in (ffn_w_gate, ffn_w_up, ffn_w_down))
    h = _first_norm(xs, *mix_norm(0), seq)
    kv = cum_rows = cum_cols = None
    out = None
    for layer in range(depth):
        g1, g2 = mod(layer, 2), mod(layer, 5)
        if layer < n_conv:
            i = layer
            u = _glu_proj(h, conv_pw1_w[i].astype(BF16), conv_pw1_b[i].reshape(1, 2 * d))
            xs, h = _out_proj(u, conv_pw2_w[i].astype(BF16), row(conv_pw2_b[i]), xs, g1,
                              [ffn_norm(layer)], [BF16], seq,
                              conv=(conv_dw_w[i], row(conv_dw_b[i]), row(conv_ln_g[i]), row(conv_ln_b[i])),
                              name="conv_out_proj")
        else:
            i = layer - n_conv
            q = _proj(h, attn_wq[i].astype(BF16), jnp.zeros((1, d), F32), BF16,
                      scale=float(d // nh) ** -0.5 * LOG2E, name="attn_q_proj")
            o = _attention(q, kv, cum_rows, cum_cols, nb, seq, nh)
            xs, h = _out_proj(o, attn_wo[i].astype(BF16), jnp.zeros((1, d), F32), xs, g1,
                              [ffn_norm(layer)], [BF16], seq, name="attn_out_proj")
        norms, dts = after_ffn(layer)
        emit_x = layer != depth - 1
        j = layer // 2
        if layer % 2 == 0:
            outs = _dense_ffn(h, ffn_g, ffn_u, ffn_d, j, xs, g2, norms, dts, seq, emit_x=emit_x)
        else:
            ne, ff = moe_w_gate.shape[1], moe_w_gate.shape[3]
            outs = _moe_layer(h, xs, g2, norms, dts, seq, moe_router_w[j], moe_router_b[j],
                              moe_w_gate.reshape(-1, d, ff), moe_w_up.reshape(-1, d, ff),
                              moe_w_down.reshape(-1, ff, d), j * ne, emit_x=emit_x)
        if emit_x:
            xs, h = outs[0], outs[1]
        else:
            out = outs[0]
        if layer + 1 == n_conv:
            h_kv = outs[2]
            kv = _proj(h_kv, w_kvf[:, :2 * d].astype(BF16), jnp.zeros((1, 2 * d), F32), BF16,
                       name="shared_kv_proj")
            wf = jnp.zeros((d, LANES), BF16).at[:, :nh].set(w_kvf[:, 2 * d:].astype(BF16))
            fl = _proj(h_kv, wf, jnp.zeros((1, LANES), F32), F32, name="forget_logits")
            bf = jnp.zeros((1, LANES), F32).at[0, :nh].set(b_f)
            cum = _forget_cumsum(fl, bf, seq)[:, :nh].reshape(nb, seq, nh)
            cum_rows = cum.transpose(0, 2, 1).reshape(nb * nh, seq)
            cum_cols = cum_rows.reshape(nb * nh, seq, 1)
    return out.reshape(nb, seq, d)
```
